```python
import jax, jax.numpy as jnp
from jax import lax
import numpy as np

D_MODEL = 1024
BATCH = 8
SEQ = 4096
DEPTH = 1

GRID_W = 64
CTX_LEN = 256
EPS = 1e-6
HG_HEADS = 4
HG_DK = 128
HG_DV = 128
HG_WIDTH = HG_HEADS * HG_DK
CHUNK = 64
FT_GROUPS = 4
FT_GROUP_W = 128
FT_WIDTH = FT_GROUPS * FT_GROUP_W
IN_COLS = 5 * HG_WIDTH + FT_WIDTH + 2 * D_MODEL
SPLIT_POINTS = (HG_WIDTH, 2 * HG_WIDTH, 3 * HG_WIDTH, 4 * HG_WIDTH, 5 * HG_WIDTH,
                5 * HG_WIDTH + FT_WIDTH, 5 * HG_WIDTH + FT_WIDTH + D_MODEL)
PEER_HEADS = 8
PEER_NKEYS = 128
PEER_EXPERTS = PEER_NKEYS * PEER_NKEYS
PEER_DKEY = 256
PEER_TOPK = 16
PEER_TOKEN_BLOCK = 128

kernel_name = "hgrn2_fnet_peer_prefix_dit"


def _rmsnorm(x, g):
    x32 = x.astype(jnp.float32)
    y = x32 * lax.rsqrt(jnp.mean(x32 * x32, axis=-1, keepdims=True) + EPS)
    return y.astype(x.dtype) * g


def _modulate(x, g, shift, scale):
    return _rmsnorm(x, g) * (1.0 + scale) + shift


def _heads(t, n_heads):
    b, l, _ = t.shape
    return t.reshape(b, l, n_heads, -1).transpose(0, 2, 1, 3)


def _flip(t):
    return jnp.flip(t, axis=2)


def _forget(z, lb):
    return lb + (1.0 - lb) * jax.nn.sigmoid(_heads(z, HG_HEADS).astype(jnp.float32))


def _hgrn2_scan(q, f, v, s0):
    b_, h_, l_, dk = q.shape
    n = l_ // CHUNK
    f32 = jnp.float32
    rs = lambda t: t.reshape(b_, h_, n, CHUNK, t.shape[-1])
    q32 = rs(q.astype(f32) * (dk ** -0.5))
    logf = rs(jnp.log(f))
    k = rs(1.0 - f)
    v32 = rs(v.astype(f32))
    b = jnp.cumsum(logf, axis=3)
    b_mid = b[:, :, :, CHUNK // 2 - 1:CHUNK // 2, :]
    b_last = b[:, :, :, -1:, :]
    a = jnp.einsum('bhnik,bhnjk->bhnij', q32 * jnp.exp(b - b_mid), k * jnp.exp(b_mid - b))
    mask = jnp.tril(jnp.ones((CHUNK, CHUNK), dtype=bool))
    a = jnp.where(mask, a, 0.0)
    o_intra = jnp.einsum('bhnij,bhnjv->bhniv', a, v32)
    q_dec = q32 * jnp.exp(b)
    k_dec = k * jnp.exp(b_last - b)
    decay = jnp.exp(b_last[:, :, :, 0, :])

    def step(s, xs):
        qd, kd, vc, dc = xs
        o = jnp.einsum('bhck,bhkv->bhcv', qd, s)
        s = dc[..., None] * s + jnp.einsum('bhck,bhcv->bhkv', kd, vc)
        return s, o

    xs = (jnp.moveaxis(q_dec, 2, 0), jnp.moveaxis(k_dec, 2, 0),
          jnp.moveaxis(v32, 2, 0), jnp.moveaxis(decay, 2, 0))
    s_final, o_inter = lax.scan(step, s0.astype(f32), xs)
    o = o_intra + jnp.moveaxis(o_inter, 0, 2)
    return o.reshape(b_, h_, l_, -1), s_final


def _hgrn2_final_state(f, v):
    b = jnp.cumsum(jnp.log(f), axis=2)
    k_dec = (1.0 - f) * jnp.exp(b[:, :, -1:, :] - b)
    return jnp.einsum('bhlk,bhlv->bhkv', k_dec, v.astype(jnp.float32))


def _fourier(t):
    b_, l_, _ = t.shape
    t32 = t.astype(jnp.float32).reshape(b_, l_, FT_GROUPS, FT_GROUP_W)
    y = jnp.fft.fftn(t32, axes=(1, 3), norm='ortho').real
    return y.reshape(b_, l_, FT_WIDTH).astype(t.dtype)


def _merge(o_hg, zg, zft, zgh, zgf, hg_norm_g, w_hg_out, w_ft_out, w_out):
    dt = zg.dtype
    o = o_hg * lax.rsqrt(jnp.mean(o_hg * o_hg, axis=-1, keepdims=True) + EPS) * hg_norm_g[:, None, :]
    b_, h_, l_, dv = o.shape
    o = o.transpose(0, 2, 1, 3).reshape(b_, l_, h_ * dv).astype(dt) * jax.nn.silu(zg)
    y_hg = o @ w_hg_out
    y_ft = _fourier(zft) @ w_ft_out
    y = jax.nn.sigmoid(zgh) * y_hg + jax.nn.sigmoid(zgf) * y_ft
    return y @ w_out


def _peer(u, w_q, sub_keys, u_tab, v_tab):
    b_, l_, d = u.shape
    xt = u.reshape(-1, PEER_TOKEN_BLOCK, d)

    def block(xb):
        t = xb.shape[0]
        q = (xb @ w_q).reshape(t, PEER_HEADS, 2, PEER_DKEY // 2)
        s = jnp.einsum('thpk,hpnk->thpn', q, sub_keys).astype(jnp.float32)
        sv, si = lax.top_k(s, PEER_TOPK)
        cand = sv[:, :, 0, :, None] + sv[:, :, 1, None, :]
        cidx = si[:, :, 0, :, None] * PEER_NKEYS + si[:, :, 1, None, :]
        cv, ci = lax.top_k(cand.reshape(t, PEER_HEADS, PEER_TOPK * PEER_TOPK), PEER_TOPK)
        eidx = jnp.take_along_axis(cidx.reshape(t, PEER_HEADS, PEER_TOPK * PEER_TOPK), ci, axis=-1)
        gates = jax.nn.softmax(cv, axis=-1)
        ue = jnp.take(u_tab, eidx, axis=0)
        act = jax.nn.gelu(jnp.einsum('thkd,td->thk', ue, xb), approximate=False)
        ve = jnp.take(v_tab, eidx, axis=0)
        w = (gates * act.astype(jnp.float32)).astype(v_tab.dtype)
        return jnp.einsum('thk,thkd->td', w, ve)

    return lax.map(block, xt).reshape(b_, l_, d)


def setup_inputs(seed: int = 0) -> dict:
    key = jax.random.key(seed)
    ks = jax.random.split(key, 20)
    nrm = lambda k, shape, s: jax.random.normal(k, shape, jnp.float32) * s
    return {
        "x": nrm(ks[0], (BATCH, SEQ, D_MODEL), 1.0),
        "c": nrm(ks[1], (BATCH, D_MODEL), 1.0),
        "ctx": nrm(ks[2], (BATCH, CTX_LEN, D_MODEL), 1.0),
        "c_ctx": nrm(ks[3], (D_MODEL,), 1.0),
        "w_ada": nrm(ks[4], (DEPTH, D_MODEL, 6 * D_MODEL), 0.5 * D_MODEL ** -0.5),
        "b_ada": nrm(ks[5], (DEPTH, 6 * D_MODEL), 0.02),
        "norm_mix_g": 1.0 + nrm(ks[6], (DEPTH, D_MODEL), 0.02),
        "norm_ffn_g": 1.0 + nrm(ks[7], (DEPTH, D_MODEL), 0.02),
        "w_in": nrm(ks[8], (DEPTH, D_MODEL, IN_COLS), D_MODEL ** -0.5),
        "hg_lb_f": nrm(ks[9], (DEPTH + 1, HG_WIDTH), 0.1),
        "hg_lb_b": nrm(ks[10], (DEPTH + 1, HG_WIDTH), 0.1),
        "hg_norm_g": 1.0 + nrm(ks[11], (DEPTH, HG_HEADS, HG_DV), 0.02),
        "w_hg_out": nrm(ks[12], (DEPTH, HG_WIDTH, D_MODEL), HG_WIDTH ** -0.5),
        "w_ft_out": nrm(ks[13], (DEPTH, FT_WIDTH, D_MODEL), FT_WIDTH ** -0.5),
        "w_out": nrm(ks[14], (DEPTH, D_MODEL, D_MODEL), D_MODEL ** -0.5),
        "peer_w_q": nrm(ks[15], (DEPTH, D_MODEL, PEER_HEADS * PEER_DKEY), D_MODEL ** -0.5),
        "peer_sub_keys": nrm(ks[16], (DEPTH, PEER_HEADS, 2, PEER_NKEYS, PEER_DKEY // 2), (PEER_DKEY // 2) ** -0.5),
        "peer_u": nrm(ks[17], (DEPTH, PEER_EXPERTS, D_MODEL), D_MODEL ** -0.5),
        "peer_v": nrm(ks[18], (DEPTH, PEER_EXPERTS, D_MODEL), PEER_HEADS ** -0.5),
        "final_norm_g": 1.0 + nrm(ks[19], (D_MODEL,), 0.02),
    }


def reference(x, c, ctx, c_ctx, w_ada, b_ada, norm_mix_g, norm_ffn_g, w_in, hg_lb_f, hg_lb_b,
              hg_norm_g, w_hg_out, w_ft_out, w_out, peer_w_q, peer_sub_keys, peer_u, peer_v,
              final_norm_g):
    lb_f_all = jnp.cumsum(jax.nn.softmax(hg_lb_f.astype(jnp.float32), axis=0), axis=0)
    lb_b_all = jnp.cumsum(jax.nn.softmax(hg_lb_b.astype(jnp.float32), axis=0), axis=0)
    h, hc = x, ctx
    for l in range(DEPTH):
        lb_f = lb_f_all[l].reshape(HG_HEADS, 1, HG_DK)
        lb_b = lb_b_all[l].reshape(HG_HEADS, 1, HG_DK)
        mod = jax.nn.silu(c) @ w_ada[l] + b_ada[l]
        sh1, sc1, g1, sh2, sc2, g2 = jnp.split(mod[:, None, :], 6, axis=-1)
        modc = jax.nn.silu(c_ctx) @ w_ada[l] + b_ada[l]
        sh1c, sc1c, g1c, sh2c, sc2c, g2c = jnp.split(modc, 6)

        uc = _modulate(hc, norm_mix_g[l], sh1c, sc1c)
        if l < DEPTH - 1:
            qc, ffc, fbc, ic, gc, ftc, ghc, gfc = jnp.split(uc @ w_in[l], SPLIT_POINTS, axis=-1)
            ffc, fbc = _forget(ffc, lb_f), _forget(fbc, lb_b)
            vc, qch = _heads(ic, HG_HEADS), _heads(qc, HG_HEADS)
            zeros = jnp.zeros((hc.shape[0], HG_HEADS, HG_DK, HG_DV), jnp.float32)
            oc_f, s_f = _hgrn2_scan(qch, ffc, vc, zeros)
            oc_b, s_b = _hgrn2_scan(_flip(qch), _flip(fbc), _flip(vc), zeros)
            hc = hc + g1c * _merge(oc_f + _flip(oc_b), gc, ftc, ghc, gfc,
                                   hg_norm_g[l], w_hg_out[l], w_ft_out[l], w_out[l])
            hc = hc + g2c * _peer(_modulate(hc, norm_ffn_g[l], sh2c, sc2c),
                                  peer_w_q[l], peer_sub_keys[l], peer_u[l], peer_v[l])
        else:
            ffc, fbc, ic = jnp.split(uc @ w_in[l][:, HG_WIDTH:4 * HG_WIDTH], 3, axis=-1)
            vc = _heads(ic, HG_HEADS)
            s_f = _hgrn2_final_state(_forget(ffc, lb_f), vc)
            s_b = _hgrn2_final_state(_flip(_forget(fbc, lb_b)), _flip(vc))

        u = _modulate(h, norm_mix_g[l], sh1, sc1)
        q, ff, fb, i, g, ft, gh, gf = jnp.split(u @ w_in[l], SPLIT_POINTS, axis=-1)
        qh, vh = _heads(q, HG_HEADS), _heads(i, HG_HEADS)
        o_f, _ = _hgrn2_scan(qh, _forget(ff, lb_f), vh, s_f)
        o_b, _ = _hgrn2_scan(_flip(qh), _flip(_forget(fb, lb_b)), _flip(vh), s_b)
        h = h + g1 * _merge(o_f + _flip(o_b), g, ft, gh, gf,
                            hg_norm_g[l], w_hg_out[l], w_ft_out[l], w_out[l])
        h = h + g2 * _peer(_modulate(h, norm_ffn_g[l], sh2, sc2),
                           peer_w_q[l], peer_sub_keys[l], peer_u[l], peer_v[l])
    return _rmsnorm(h, final_norm_g)
```

```python
import functools
import math

import jax
import jax.numpy as jnp
from jax import lax
from jax.experimental import pallas as pl
from jax.experimental.pallas import tpu as pltpu

F32 = jnp.float32
BF16 = jnp.bfloat16
I32 = jnp.int32

EPS = 1e-6
HG_HEADS = 4
HG_DK = 128
HG_WIDTH = HG_HEADS * HG_DK
CHUNK = 64
FT_GROUPS = 4
FT_GROUP_W = 128
FT_WIDTH = FT_GROUPS * FT_GROUP_W
PEER_HEADS = 8
PEER_NKEYS = 128
PEER_TOPK = 16
PEER_SLOTS = PEER_HEADS * PEER_TOPK
LANES = 128
SUBLANES = 8
ROW_WORDS = 4
VMEM_LIMIT = 56 * 1024 * 1024


def _sigmoid(x):
    return 1.0 / (1.0 + jnp.exp(-x))


def _silu(x):
    return x * _sigmoid(x)


def _dot(a, b):
    return jnp.dot(a, b, preferred_element_type=F32)


def _dot_nt(a, b):
    return lax.dot_general(a, b, (((1,), (1,)), ((), ())), preferred_element_type=F32)


def _split_bf16(x):
    hi = x.astype(BF16)
    lo = (x - hi.astype(F32)).astype(BF16)
    return hi, lo


def _params(sem, limit=VMEM_LIMIT):
    return pltpu.CompilerParams(dimension_semantics=sem, vmem_limit_bytes=limit)


def _ada_kernel(c_ref, w_ref, b_ref, o_ref):
    a = _silu(c_ref[...])
    a_hi, a_lo = _split_bf16(a)
    w_hi, w_lo = _split_bf16(w_ref[...])
    o_ref[...] = _dot(a_hi, w_hi) + _dot(a_lo, w_hi) + _dot(a_hi, w_lo) + b_ref[...]


def _ada(cc, w, b):
    rows, d = cc.shape
    cols = w.shape[1]
    tn = 1024
    return pl.pallas_call(
        _ada_kernel,
        grid=(cols // tn,),
        in_specs=[pl.BlockSpec((rows, d), lambda j: (0, 0)),
                  pl.BlockSpec((d, tn), lambda j: (0, j)),
                  pl.BlockSpec((1, tn), lambda j: (0, j))],
        out_specs=pl.BlockSpec((rows, tn), lambda j: (0, j)),
        out_shape=jax.ShapeDtypeStruct((rows, cols), F32),
        compiler_params=_params(("parallel",)),
        name="ada",
    )(cc, w, b)


def _modulated_norm(x, g, shift, scale):
    y = x * lax.rsqrt(jnp.mean(x * x, axis=-1, keepdims=True) + EPS)
    return y * g * (1.0 + scale) + shift


def _normmm_kernel(h_ref, sh_ref, sc_ref, g_ref, w_ref, o_ref, *, tn):
    u = _modulated_norm(h_ref[...], g_ref[...], sh_ref[...], sc_ref[...]).astype(BF16)
    for j in range(o_ref.shape[-1] // tn):
        o_ref[:, j * tn:(j + 1) * tn] = _dot(u, w_ref[:, j * tn:(j + 1) * tn])


def _normmm(h, shift, scale, g, w, tm):
    bsz, l, d = h.shape
    c = w.shape[1]
    return pl.pallas_call(
        functools.partial(_normmm_kernel, tn=512),
        grid=(bsz, l // tm),
        in_specs=[pl.BlockSpec((None, tm, d), lambda b, i: (b, i, 0)),
                  pl.BlockSpec((None, 1, d), lambda b, i: (b, 0, 0)),
                  pl.BlockSpec((None, 1, d), lambda b, i: (b, 0, 0)),
                  pl.BlockSpec((1, d), lambda b, i: (0, 0)),
                  pl.BlockSpec((d, c), lambda b, i: (0, 0))],
        out_specs=pl.BlockSpec((None, tm, c), lambda b, i: (b, i, 0)),
        out_shape=jax.ShapeDtypeStruct((bsz, l, c), F32),
        compiler_params=_params(("parallel", "parallel")),
        name="normmm",
    )(h, shift, scale, g, w)


def _scan_kernel(qf_ref, ff_ref, vf_ref, qb_ref, fb_ref, vb_ref, cff_ref, cfb_ref, cv_ref,
                 lbf_ref, lbb_ref, of_ref, ob_ref, sf_ref, sb_ref, *, tl, ctx_len):
    j = pl.program_id(2)
    c_ = CHUNK
    row = lax.broadcasted_iota(I32, (c_, c_), 0)
    col = lax.broadcasted_iota(I32, (c_, c_), 1)
    tril = col <= row
    triu = col >= row
    tril_b = tril.astype(BF16)
    triu_b = triu.astype(BF16)
    lbf = lbf_ref[...]
    lbb = lbb_ref[...]
    qscale = HG_DK ** -0.5

    def decays(fz, lb, fwd):
        f = lb + (1.0 - lb) * _sigmoid(fz)
        hi, lo = _split_bf16(jnp.log(f))
        tri = tril_b if fwd else triu_b
        b = _dot(tri, hi) + _dot(tri, lo)
        if fwd:
            return f, b, b[c_ // 2 - 1:c_ // 2], b[c_ - 1:c_]
        return f, b, b[c_ // 2:c_ // 2 + 1], b[0:1]

    def next_state(s_t, f, b, b_last, v):
        kd = (1.0 - f) * jnp.exp(b_last - b)
        return jnp.exp(b_last) * s_t + _dot(v.T.astype(BF16), kd.astype(BF16))

    def chunk(q, fz, v, lb, s_t, fwd):
        f, b, b_mid, b_last = decays(fz, lb, fwd)
        qs = q * qscale
        qt = qs * jnp.exp(b - b_mid)
        kt = (1.0 - f) * jnp.exp(b_mid - b)
        a = _dot_nt(qt.astype(BF16), kt.astype(BF16))
        a = jnp.where(tril if fwd else triu, a, 0.0)
        o = _dot(a.astype(BF16), v.astype(BF16))
        o = o + _dot_nt((qs * jnp.exp(b)).astype(BF16), s_t.astype(BF16))
        return o, next_state(s_t, f, b, b_last, v)

    @pl.when(j == 0)
    def _():
        s_f = jnp.zeros((HG_DK, HG_DK), F32)
        s_b = jnp.zeros((HG_DK, HG_DK), F32)
        n = ctx_len // c_
        for c in range(n):
            f, b, _, b_last = decays(cff_ref[c * c_:(c + 1) * c_, :], lbf, True)
            s_f = next_state(s_f, f, b, b_last, cv_ref[c * c_:(c + 1) * c_, :])
            cb = n - 1 - c
            f, b, _, b_last = decays(cfb_ref[cb * c_:(cb + 1) * c_, :], lbb, False)
            s_b = next_state(s_b, f, b, b_last, cv_ref[cb * c_:(cb + 1) * c_, :])
        sf_ref[...] = s_f
        sb_ref[...] = s_b

    n_chunks = tl // c_

    def step(c, carry):
        r = pl.multiple_of(c * c_, c_)
        o, s = chunk(qf_ref[pl.ds(r, c_), :], ff_ref[pl.ds(r, c_), :], vf_ref[pl.ds(r, c_), :],
                     lbf, sf_ref[...], True)
        of_ref[pl.ds(r, c_), :] = o
        sf_ref[...] = s
        rb = pl.multiple_of((n_chunks - 1 - c) * c_, c_)
        o, s = chunk(qb_ref[pl.ds(rb, c_), :], fb_ref[pl.ds(rb, c_), :], vb_ref[pl.ds(rb, c_), :],
                     lbb, sb_ref[...], False)
        ob_ref[pl.ds(rb, c_), :] = o
        sb_ref[...] = s
        return carry

    lax.fori_loop(0, n_chunks, step, 0)


def _scan(z, zc, lb_f, lb_b, tl):
    bsz, l, _ = z.shape
    ctx_len = zc.shape[1]
    nj = l // tl
    hb = HG_HEADS
    blk = lambda colgrp, rev: pl.BlockSpec(
        (None, tl, HG_DK),
        (lambda b, h, j: (b, nj - 1 - j, colgrp * hb + h)) if rev else (lambda b, h, j: (b, j, colgrp * hb + h)))
    cblk = lambda colgrp: pl.BlockSpec((None, ctx_len, HG_DK), lambda b, h, j: (b, 0, colgrp * hb + h))
    lbspec = pl.BlockSpec((1, HG_DK), lambda b, h, j: (0, h))
    ospec = lambda rev: pl.BlockSpec(
        (None, tl, HG_DK), (lambda b, h, j: (b, nj - 1 - j, h)) if rev else (lambda b, h, j: (b, j, h)))
    return pl.pallas_call(
        functools.partial(_scan_kernel, tl=tl, ctx_len=ctx_len),
        grid=(bsz, hb, nj),
        in_specs=[blk(0, False), blk(1, False), blk(3, False),
                  blk(0, True), blk(2, True), blk(3, True),
                  cblk(0), cblk(1), cblk(2), lbspec, lbspec],
        out_specs=[ospec(False), ospec(True)],
        out_shape=[jax.ShapeDtypeStruct((bsz, l, HG_WIDTH), F32)] * 2,
        scratch_shapes=[pltpu.VMEM((HG_DK, HG_DK), F32), pltpu.VMEM((HG_DK, HG_DK), F32)],
        compiler_params=_params(("parallel", "parallel", "arbitrary")),
        name="scan",
    )(z, z, z, z, z, z, zc, zc, zc, lb_f, lb_b)


def _cdft_kernel(x_ref, w_ref, o_ref):
    o_ref[...] = _dot(x_ref[...].astype(BF16), w_ref[...]).astype(BF16)


def _cdft(z, wc, tm):
    bsz, l, _ = z.shape
    ftblk = 5 * HG_WIDTH // FT_WIDTH
    return pl.pallas_call(
        _cdft_kernel,
        grid=(bsz, l // tm),
        in_specs=[pl.BlockSpec((None, tm, FT_WIDTH), lambda b, i: (b, i, ftblk)),
                  pl.BlockSpec((FT_WIDTH, 2 * FT_WIDTH), lambda b, i: (0, 0))],
        out_specs=pl.BlockSpec((tm, 2 * FT_WIDTH), lambda b, i: (i, b)),
        out_shape=jax.ShapeDtypeStruct((l, bsz * 2 * FT_WIDTH), BF16),
        compiler_params=_params(("parallel", "parallel")),
        name="cdft",
    )(z, wc)


def _pdft_kernel(cl_ref, sl_ref, xc_ref, xs_ref, o_ref):
    o_ref[...] = (_dot(cl_ref[...], xc_ref[...]) + _dot(sl_ref[...], xs_ref[...])).astype(BF16)


def _pdft(cl, snl, r, tm):
    l = cl.shape[0]
    nb = r.shape[1] // (2 * FT_WIDTH)
    return pl.pallas_call(
        _pdft_kernel,
        grid=(l // tm, nb),
        in_specs=[pl.BlockSpec((tm, l), lambda i, n: (i, 0)),
                  pl.BlockSpec((tm, l), lambda i, n: (i, 0)),
                  pl.BlockSpec((l, FT_WIDTH), lambda i, n: (0, 2 * n)),
                  pl.BlockSpec((l, FT_WIDTH), lambda i, n: (0, 2 * n + 1))],
        out_specs=pl.BlockSpec((tm, FT_WIDTH), lambda i, n: (i, n)),
        out_shape=jax.ShapeDtypeStruct((l, nb * FT_WIDTH), BF16),
        compiler_params=_params(("parallel", "arbitrary")),
        name="pdft",
    )(cl, snl, r, r)


def _dft_tables(l):
    w = FT_GROUP_W
    jk = (jnp.arange(w, dtype=I32)[:, None] * jnp.arange(w, dtype=I32)[None, :]) % w
    ang = jk.astype(F32) * (2.0 * math.pi / w)
    norm = 1.0 / math.sqrt(l * w)
    eye = jnp.eye(FT_GROUPS, dtype=F32)
    wc = jnp.concatenate([jnp.kron(eye, jnp.cos(ang) * norm), jnp.kron(eye, jnp.sin(ang) * norm)], axis=1)
    jl = (jnp.arange(l, dtype=I32)[:, None] * jnp.arange(l, dtype=I32)[None, :]) % l
    angl = jl.astype(F32) * (2.0 * math.pi / l)
    return wc.astype(BF16), jnp.cos(angl).astype(BF16), (-jnp.sin(angl)).astype(BF16)


def _merge_kernel(of_ref, ob_ref, zg_ref, zgh_ref, zgf_ref, yft_ref, x_ref, g1_ref, hgn_ref,
                  whg_ref, wft_ref, wout_ref, o_ref):
    o = of_ref[...] + ob_ref[...]
    hgn = hgn_ref[...]
    parts = []
    for h in range(HG_HEADS):
        oh = o[:, h * HG_DK:(h + 1) * HG_DK]
        ms = jnp.mean(oh * oh, axis=-1, keepdims=True)
        parts.append(oh * lax.rsqrt(ms + EPS) * hgn[:, h * HG_DK:(h + 1) * HG_DK])
    on = jnp.concatenate(parts, axis=1) * _silu(zg_ref[...])
    y_hg = _dot(on.astype(BF16), whg_ref[...])
    y_ft = _dot(yft_ref[...], wft_ref[...])
    y = _sigmoid(zgh_ref[...]) * y_hg + _sigmoid(zgf_ref[...]) * y_ft
    o_ref[...] = x_ref[...] + g1_ref[...] * _dot(y.astype(BF16), wout_ref[...])


def _merge(o_f, o_b, z, yft, x, g1, hgn, whg, wft, wout, tm):
    bsz, l, d = x.shape
    full = lambda shape: pl.BlockSpec(shape, lambda b, i: (0,) * len(shape))
    return pl.pallas_call(
        _merge_kernel,
        grid=(bsz, l // tm),
        in_specs=[pl.BlockSpec((None, tm, HG_WIDTH), lambda b, i: (b, i, 0)),
                  pl.BlockSpec((None, tm, HG_WIDTH), lambda b, i: (b, i, 0)),
                  pl.BlockSpec((None, tm, HG_WIDTH), lambda b, i: (b, i, 4)),
                  pl.BlockSpec((None, tm, d), lambda b, i: (b, i, 3)),
                  pl.BlockSpec((None, tm, d), lambda b, i: (b, i, 4)),
                  pl.BlockSpec((tm, FT_WIDTH), lambda b, i: (i, b)),
                  pl.BlockSpec((None, tm, d), lambda b, i: (b, i, 0)),
                  pl.BlockSpec((None, 1, d), lambda b, i: (b, 0, 0)),
                  full((1, HG_WIDTH)), full((HG_WIDTH, d)), full((FT_WIDTH, d)), full((d, d))],
        out_specs=pl.BlockSpec((None, tm, d), lambda b, i: (b, i, 0)),
        out_shape=jax.ShapeDtypeStruct((bsz, l, d), F32),
        compiler_params=_params(("parallel", "parallel")),
        name="merge",
    )(o_f, o_b, z, z, z, yft, x, g1, hgn, whg, wft, wout)


def _topk_rows(s, k, payload=None):
    n = s.shape[0]
    iota = lax.broadcasted_iota(I32, s.shape, 0)
    vals, sel = [], []
    for _ in range(k):
        m = jnp.max(s, axis=0, keepdims=True)
        am = jnp.min(jnp.where(s == m, iota, n), axis=0, keepdims=True)
        hit = iota == am
        vals.append(m)
        sel.append(am if payload is None else jnp.max(jnp.where(hit, payload, -1), axis=0, keepdims=True))
        s = jnp.where(hit, -jnp.inf, s)
    return jnp.concatenate(vals, axis=0), jnp.concatenate(sel, axis=0)


def _route_kernel(h_ref, sh_ref, sc_ref, g_ref, wq_ref, keys_ref, u_ref, e_ref, gate_ref):
    u = _modulated_norm(h_ref[...], g_ref[...], sh_ref[...], sc_ref[...])
    u_ref[...] = u
    q = _dot(u.astype(BF16), wq_ref[...]).astype(BF16)
    k_ = PEER_TOPK
    e_rows, g_rows = [], []
    for h in range(PEER_HEADS):
        sv, si = [], []
        for p in range(2):
            hp = 2 * h + p
            s_t = _dot_nt(keys_ref[hp], q[:, hp * LANES:(hp + 1) * LANES])
            v, i = _topk_rows(s_t, k_)
            sv.append(v)
            si.append(i)
        cand = jnp.concatenate([sv[0][a:a + 1] + sv[1] for a in range(k_)], axis=0)
        cidx = jnp.concatenate([si[0][a:a + 1] * PEER_NKEYS + si[1] for a in range(k_)], axis=0)
        cv, ce = _topk_rows(cand, k_, payload=cidx)
        ex = jnp.exp(cv - cv[0:1])
        g_rows.append(ex / jnp.sum(ex, axis=0, keepdims=True))
        e_rows.append(ce)
    e_ref[...] = jnp.concatenate(e_rows, axis=0).T
    gate_ref[...] = jnp.concatenate(g_rows, axis=0).T


def _route(h1, sh, sc, g, wq, keys, tm):
    bsz, l, d = h1.shape
    full = lambda shape: pl.BlockSpec(shape, lambda b, i: (0,) * len(shape))
    return pl.pallas_call(
        _route_kernel,
        grid=(bsz, l // tm),
        in_specs=[pl.BlockSpec((None, tm, d), lambda b, i: (b, i, 0)),
                  pl.BlockSpec((None, 1, d), lambda b, i: (b, 0, 0)),
                  pl.BlockSpec((None, 1, d), lambda b, i: (b, 0, 0)),
                  full((1, d)), full(wq.shape), full(keys.shape)],
        out_specs=[pl.BlockSpec((None, tm, d), lambda b, i: (b, i, 0)),
                   pl.BlockSpec((None, tm, PEER_SLOTS), lambda b, i: (b, i, 0)),
                   pl.BlockSpec((None, tm, PEER_SLOTS), lambda b, i: (b, i, 0))],
        out_shape=[jax.ShapeDtypeStruct((bsz, l, d), F32),
                   jax.ShapeDtypeStruct((bsz, l, PEER_SLOTS), I32),
                   jax.ShapeDtypeStruct((bsz, l, PEER_SLOTS), F32)],
        compiler_params=_params(("parallel", "parallel")),
        name="route",
    )(h1, sh, sc, g, wq, keys)


def _pack_table(tab):
    e, d = tab.shape
    half = d // 2
    bits = lax.bitcast_convert_type(tab.astype(BF16), jnp.uint16).astype(jnp.uint32)
    word = bits[:, :half] | (bits[:, half:] << 16)
    return lax.bitcast_convert_type(word, I32).reshape(e * ROW_WORDS, LANES)


def _unpack_row(tab_ref, e):
    w = tab_ref[pl.ds(pl.multiple_of(e * ROW_WORDS, ROW_WORDS), ROW_WORDS), :]
    lo = lax.bitcast_convert_type(w << 16, F32)
    hi = lax.bitcast_convert_type(w & jnp.int32(-65536), F32)
    return lo, hi


def _act_kernel(e_ref, gate_ref, x_ref, tab_ref, w_ref, r_ref, *, tb):
    ones = jnp.ones((SUBLANES, LANES), BF16)

    def token(t, carry):
        xt = x_ref[t]
        xlo = xt[0:ROW_WORDS]
        xhi = xt[ROW_WORDS:]
        for j in range(PEER_SLOTS):
            lo, hi = _unpack_row(tab_ref, e_ref[t, j])
            p = lo * xlo + hi * xhi
            r_ref[j:j + 1, :] = jnp.sum(p, axis=0, keepdims=True)
        r_hi, r_lo = _split_bf16(r_ref[...])
        act = (_dot_nt(ones, r_hi) + _dot_nt(ones, r_lo))[0:1]
        gelu = 0.5 * act * (1.0 + lax.erf(act * (2.0 ** -0.5)))
        w_ref[pl.ds(t, 1), :] = gate_ref[pl.ds(t, 1), :] * gelu
        return carry

    lax.fori_loop(0, tb, token, 0)


def _act(eidx, gates, x8, tab, tb):
    n = eidx.shape[0]
    return pl.pallas_call(
        functools.partial(_act_kernel, tb=tb),
        grid=(n // tb,),
        in_specs=[pl.BlockSpec((tb, PEER_SLOTS), lambda i: (i, 0), memory_space=pltpu.SMEM),
                  pl.BlockSpec((tb, PEER_SLOTS), lambda i: (i, 0)),
                  pl.BlockSpec((tb, SUBLANES, LANES), lambda i: (i, 0, 0)),
                  pl.BlockSpec(tab.shape, lambda i: (0, 0), pipeline_mode=pl.Buffered(1))],
        out_specs=pl.BlockSpec((tb, PEER_SLOTS), lambda i: (i, 0)),
        out_shape=jax.ShapeDtypeStruct((n, PEER_SLOTS), F32),
        scratch_shapes=[pltpu.VMEM((PEER_SLOTS, LANES), F32)],
        compiler_params=_params(("arbitrary",)),
        name="peer_act",
    )(eidx, gates, x8, tab)


def _out_kernel(e_ref, w_ref, h_ref, g2_ref, fg_ref, tab_ref, o_ref, *, tb, n_acc):
    def token(t, carry):
        acc_lo = [jnp.zeros((ROW_WORDS, LANES), F32) for _ in range(n_acc)]
        acc_hi = [jnp.zeros((ROW_WORDS, LANES), F32) for _ in range(n_acc)]
        for j in range(PEER_SLOTS):
            lo, hi = _unpack_row(tab_ref, e_ref[t, j])
            wj = w_ref[t, j]
            acc_lo[j % n_acc] = acc_lo[j % n_acc] + wj * lo
            acc_hi[j % n_acc] = acc_hi[j % n_acc] + wj * hi
        y = jnp.concatenate([sum(acc_lo[1:], acc_lo[0]), sum(acc_hi[1:], acc_hi[0])], axis=0)
        h2 = h_ref[t] + g2_ref[...] * y
        ms = jnp.sum(h2 * h2) * (1.0 / (SUBLANES * LANES))
        o_ref[t] = h2 * lax.rsqrt(ms + EPS) * fg_ref[...]
        return carry

    lax.fori_loop(0, tb, token, 0)


def _peer_out(eidx, w, h8, g2_8, fg8, tab, tb, tokens_per_batch):
    n = eidx.shape[0]
    return pl.pallas_call(
        functools.partial(_out_kernel, tb=tb, n_acc=4),
        grid=(n // tb,),
        in_specs=[pl.BlockSpec((tb, PEER_SLOTS), lambda i: (i, 0), memory_space=pltpu.SMEM),
                  pl.BlockSpec((tb, PEER_SLOTS), lambda i: (i, 0), memory_space=pltpu.SMEM),
                  pl.BlockSpec((tb, SUBLANES, LANES), lambda i: (i, 0, 0)),
                  pl.BlockSpec((None, SUBLANES, LANES), lambda i: ((i * tb) // tokens_per_batch, 0, 0)),
                  pl.BlockSpec((SUBLANES, LANES), lambda i: (0, 0)),
                  pl.BlockSpec(tab.shape, lambda i: (0, 0), pipeline_mode=pl.Buffered(1))],
        out_specs=pl.BlockSpec((tb, SUBLANES, LANES), lambda i: (i, 0, 0)),
        out_shape=jax.ShapeDtypeStruct((n, SUBLANES, LANES), F32),
        compiler_params=_params(("arbitrary",)),
        name="peer_out",
    )(eidx, w, h8, g2_8, fg8, tab)


def kernel(x, c, ctx, c_ctx, w_ada, b_ada, norm_mix_g, norm_ffn_g, w_in, hg_lb_f, hg_lb_b, hg_norm_g,
           w_hg_out, w_ft_out, w_out, peer_w_q, peer_sub_keys, peer_u, peer_v, final_norm_g):
    bsz, l, d = x.shape
    n = bsz * l
    layer = 0

    lb_f = jnp.cumsum(jax.nn.softmax(hg_lb_f.astype(F32), axis=0), axis=0)[layer][None, :]
    lb_b = jnp.cumsum(jax.nn.softmax(hg_lb_b.astype(F32), axis=0), axis=0)[layer][None, :]

    rows = -(-(bsz + 1) // SUBLANES) * SUBLANES
    cc = jnp.zeros((rows, d), F32).at[:bsz].set(c).at[bsz].set(c_ctx)
    mod = _ada(cc, w_ada[layer], b_ada[layer][None, :])
    sh1, sc1, g1, sh2, sc2, g2 = [mod[:bsz, k * d:(k + 1) * d][:, None, :] for k in range(6)]
    sh1c, sc1c = [jnp.broadcast_to(mod[bsz, k * d:(k + 1) * d][None, None, :], (bsz, 1, d)) for k in range(2)]

    g_mix = norm_mix_g[layer][None, :]
    w_in_b = w_in[layer].astype(BF16)
    z = _normmm(x, sh1, sc1, g_mix, w_in_b, tm=256)
    zc = _normmm(ctx, sh1c, sc1c, g_mix, w_in_b[:, HG_WIDTH:4 * HG_WIDTH], tm=ctx.shape[1])

    o_f, o_b = _scan(z, zc, lb_f, lb_b, tl=512)

    wc, cl, snl = _dft_tables(l)
    yft = _pdft(cl, snl, _cdft(z, wc, tm=512), tm=512)

    h1 = _merge(o_f, o_b, z, yft, x, g1, hg_norm_g[layer].reshape(1, HG_WIDTH),
                w_hg_out[layer].astype(BF16), w_ft_out[layer].astype(BF16), w_out[layer].astype(BF16), tm=512)

    keys = peer_sub_keys[layer].reshape(PEER_HEADS * 2, PEER_NKEYS, -1).astype(BF16)
    u2, eidx, gates = _route(h1, sh2, sc2, norm_ffn_g[layer][None, :], peer_w_q[layer].astype(BF16), keys, tm=256)

    eidx = eidx.reshape(n, PEER_SLOTS)
    w = _act(eidx, gates.reshape(n, PEER_SLOTS), u2.reshape(n, SUBLANES, LANES), _pack_table(peer_u[layer]), tb=64)
    out = _peer_out(eidx, w, h1.reshape(n, SUBLANES, LANES), g2.reshape(bsz, SUBLANES, LANES),
                    final_norm_g.reshape(SUBLANES, LANES), _pack_table(peer_v[layer]), tb=64, tokens_per_batch=l)
    return out.reshape(bsz, l, d)
```

```python
import functools
import math

import jax
import jax.numpy as jnp
from jax import lax
from jax.experimental import pallas as pl
from jax.experimental.pallas import tpu as pltpu

F32 = jnp.float32
BF16 = jnp.bfloat16
I32 = jnp.int32

EPS = 1e-6
HG_HEADS = 4
HG_DK = 128
HG_WIDTH = HG_HEADS * HG_DK
CHUNK = 64
FT_GROUPS = 4
FT_GROUP_W = 128
FT_WIDTH = FT_GROUPS * FT_GROUP_W
PEER_HEADS = 8
PEER_NKEYS = 128
PEER_TOPK = 16
PEER_SLOTS = PEER_HEADS * PEER_TOPK
LANES = 128
SUBLANES = 8
ROW_WORDS = 4
GATHER_BUFS = 4
PEER_TB = 128
VMEM_LIMIT = 56 * 1024 * 1024


def _sigmoid(x):
    return 1.0 / (1.0 + jnp.exp(-x))


def _silu(x):
    return x * _sigmoid(x)


def _dot(a, b):
    return jnp.dot(a, b, preferred_element_type=F32)


def _dot_nt(a, b):
    return lax.dot_general(a, b, (((1,), (1,)), ((), ())), preferred_element_type=F32)


def _split_bf16(x):
    hi = x.astype(BF16)
    lo = (x - hi.astype(F32)).astype(BF16)
    return hi, lo


def _params(sem, limit=VMEM_LIMIT):
    return pltpu.CompilerParams(dimension_semantics=sem, vmem_limit_bytes=limit)


def _ada_kernel(c_ref, w_ref, b_ref, o_ref):
    a = _silu(c_ref[...])
    a_hi, a_lo = _split_bf16(a)
    w_hi, w_lo = _split_bf16(w_ref[...])
    o_ref[...] = _dot(a_hi, w_hi) + _dot(a_lo, w_hi) + _dot(a_hi, w_lo) + b_ref[...]


def _ada(cc, w, b):
    rows, d = cc.shape
    cols = w.shape[1]
    tn = 1024
    return pl.pallas_call(
        _ada_kernel,
        grid=(cols // tn,),
        in_specs=[pl.BlockSpec((rows, d), lambda j: (0, 0)),
                  pl.BlockSpec((d, tn), lambda j: (0, j)),
                  pl.BlockSpec((1, tn), lambda j: (0, j))],
        out_specs=pl.BlockSpec((rows, tn), lambda j: (0, j)),
        out_shape=jax.ShapeDtypeStruct((rows, cols), F32),
        compiler_params=_params(("parallel",)),
        name="ada",
    )(cc, w, b)


def _modulated_norm(x, g, shift, scale):
    y = x * lax.rsqrt(jnp.mean(x * x, axis=-1, keepdims=True) + EPS)
    return y * g * (1.0 + scale) + shift


def _normmm_kernel(h_ref, sh_ref, sc_ref, g_ref, w_ref, o_ref, *, tn):
    u = _modulated_norm(h_ref[...], g_ref[...], sh_ref[...], sc_ref[...]).astype(BF16)
    for j in range(o_ref.shape[-1] // tn):
        o_ref[:, j * tn:(j + 1) * tn] = _dot(u, w_ref[:, j * tn:(j + 1) * tn])


def _normmm(h, shift, scale, g, w, tm):
    bsz, l, d = h.shape
    c = w.shape[1]
    return pl.pallas_call(
        functools.partial(_normmm_kernel, tn=512),
        grid=(bsz, l // tm),
        in_specs=[pl.BlockSpec((None, tm, d), lambda b, i: (b, i, 0)),
                  pl.BlockSpec((None, 1, d), lambda b, i: (b, 0, 0)),
                  pl.BlockSpec((None, 1, d), lambda b, i: (b, 0, 0)),
                  pl.BlockSpec((1, d), lambda b, i: (0, 0)),
                  pl.BlockSpec((d, c), lambda b, i: (0, 0))],
        out_specs=pl.BlockSpec((None, tm, c), lambda b, i: (b, i, 0)),
        out_shape=jax.ShapeDtypeStruct((bsz, l, c), F32),
        compiler_params=_params(("parallel", "parallel")),
        name="normmm",
    )(h, shift, scale, g, w)


def _scan_kernel(qf_ref, ff_ref, vf_ref, qb_ref, fb_ref, vb_ref, cff_ref, cfb_ref, cv_ref,
                 lbf_ref, lbb_ref, of_ref, ob_ref, sf_ref, sb_ref, *, tl, ctx_len):
    j = pl.program_id(2)
    c_ = CHUNK
    row = lax.broadcasted_iota(I32, (c_, c_), 0)
    col = lax.broadcasted_iota(I32, (c_, c_), 1)
    tril = col <= row
    triu = col >= row
    tril_b = tril.astype(BF16)
    triu_b = triu.astype(BF16)
    lbf = lbf_ref[...]
    lbb = lbb_ref[...]
    qscale = HG_DK ** -0.5

    def decays(fz, lb, fwd):
        f = lb + (1.0 - lb) * _sigmoid(fz)
        hi, lo = _split_bf16(jnp.log(f))
        tri = tril_b if fwd else triu_b
        b = _dot(tri, hi) + _dot(tri, lo)
        if fwd:
            return f, b, b[c_ // 2 - 1:c_ // 2], b[c_ - 1:c_]
        return f, b, b[c_ // 2:c_ // 2 + 1], b[0:1]

    def next_state(s_t, f, b, b_last, v):
        kd = (1.0 - f) * jnp.exp(b_last - b)
        return jnp.exp(b_last) * s_t + _dot(v.T.astype(BF16), kd.astype(BF16))

    def chunk(q, fz, v, lb, s_t, fwd):
        f, b, b_mid, b_last = decays(fz, lb, fwd)
        qs = q * qscale
        qt = qs * jnp.exp(b - b_mid)
        kt = (1.0 - f) * jnp.exp(b_mid - b)
        a = _dot_nt(qt.astype(BF16), kt.astype(BF16))
        a = jnp.where(tril if fwd else triu, a, 0.0)
        o = _dot(a.astype(BF16), v.astype(BF16))
        o = o + _dot_nt((qs * jnp.exp(b)).astype(BF16), s_t.astype(BF16))
        return o, next_state(s_t, f, b, b_last, v)

    @pl.when(j == 0)
    def _():
        s_f = jnp.zeros((HG_DK, HG_DK), F32)
        s_b = jnp.zeros((HG_DK, HG_DK), F32)
        n = ctx_len // c_
        for c in range(n):
            f, b, _, b_last = decays(cff_ref[c * c_:(c + 1) * c_, :], lbf, True)
            s_f = next_state(s_f, f, b, b_last, cv_ref[c * c_:(c + 1) * c_, :])
            cb = n - 1 - c
            f, b, _, b_last = decays(cfb_ref[cb * c_:(cb + 1) * c_, :], lbb, False)
            s_b = next_state(s_b, f, b, b_last, cv_ref[cb * c_:(cb + 1) * c_, :])
        sf_ref[...] = s_f
        sb_ref[...] = s_b

    n_chunks = tl // c_

    def step(c, carry):
        r = pl.multiple_of(c * c_, c_)
        o, s = chunk(qf_ref[pl.ds(r, c_), :], ff_ref[pl.ds(r, c_), :], vf_ref[pl.ds(r, c_), :],
                     lbf, sf_ref[...], True)
        of_ref[pl.ds(r, c_), :] = o
        sf_ref[...] = s
        rb = pl.multiple_of((n_chunks - 1 - c) * c_, c_)
        o, s = chunk(qb_ref[pl.ds(rb, c_), :], fb_ref[pl.ds(rb, c_), :], vb_ref[pl.ds(rb, c_), :],
                     lbb, sb_ref[...], False)
        ob_ref[pl.ds(rb, c_), :] = o
        sb_ref[...] = s
        return carry

    lax.fori_loop(0, n_chunks, step, 0)


def _scan(z, zc, lb_f, lb_b, tl):
    bsz, l, _ = z.shape
    ctx_len = zc.shape[1]
    nj = l // tl
    hb = HG_HEADS
    blk = lambda colgrp, rev: pl.BlockSpec(
        (None, tl, HG_DK),
        (lambda b, h, j: (b, nj - 1 - j, colgrp * hb + h)) if rev else (lambda b, h, j: (b, j, colgrp * hb + h)))
    cblk = lambda colgrp: pl.BlockSpec((None, ctx_len, HG_DK), lambda b, h, j: (b, 0, colgrp * hb + h))
    lbspec = pl.BlockSpec((1, HG_DK), lambda b, h, j: (0, h))
    ospec = lambda rev: pl.BlockSpec(
        (None, tl, HG_DK), (lambda b, h, j: (b, nj - 1 - j, h)) if rev else (lambda b, h, j: (b, j, h)))
    return pl.pallas_call(
        functools.partial(_scan_kernel, tl=tl, ctx_len=ctx_len),
        grid=(bsz, hb, nj),
        in_specs=[blk(0, False), blk(1, False), blk(3, False),
                  blk(0, True), blk(2, True), blk(3, True),
                  cblk(0), cblk(1), cblk(2), lbspec, lbspec],
        out_specs=[ospec(False), ospec(True)],
        out_shape=[jax.ShapeDtypeStruct((bsz, l, HG_WIDTH), F32)] * 2,
        scratch_shapes=[pltpu.VMEM((HG_DK, HG_DK), F32), pltpu.VMEM((HG_DK, HG_DK), F32)],
        compiler_params=_params(("parallel", "parallel", "arbitrary")),
        name="scan",
    )(z, z, z, z, z, z, zc, zc, zc, lb_f, lb_b)


def _cdft_kernel(x_ref, w_ref, o_ref):
    o_ref[...] = _dot(x_ref[...].astype(BF16), w_ref[...]).astype(BF16)


def _cdft(z, wc, tm):
    bsz, l, _ = z.shape
    ftblk = 5 * HG_WIDTH // FT_WIDTH
    return pl.pallas_call(
        _cdft_kernel,
        grid=(bsz, l // tm),
        in_specs=[pl.BlockSpec((None, tm, FT_WIDTH), lambda b, i: (b, i, ftblk)),
                  pl.BlockSpec((FT_WIDTH, 2 * FT_WIDTH), lambda b, i: (0, 0))],
        out_specs=pl.BlockSpec((tm, 2 * FT_WIDTH), lambda b, i: (i, b)),
        out_shape=jax.ShapeDtypeStruct((l, bsz * 2 * FT_WIDTH), BF16),
        compiler_params=_params(("parallel", "parallel")),
        name="cdft",
    )(z, wc)


def _pdft_kernel(cl_ref, sl_ref, xc_ref, xs_ref, o_ref):
    o_ref[...] = (_dot(cl_ref[...], xc_ref[...]) + _dot(sl_ref[...], xs_ref[...])).astype(BF16)


def _pdft(cl, snl, r, tm):
    l = cl.shape[0]
    nb = r.shape[1] // (2 * FT_WIDTH)
    return pl.pallas_call(
        _pdft_kernel,
        grid=(l // tm, nb),
        in_specs=[pl.BlockSpec((tm, l), lambda i, n: (i, 0)),
                  pl.BlockSpec((tm, l), lambda i, n: (i, 0)),
                  pl.BlockSpec((l, FT_WIDTH), lambda i, n: (0, 2 * n)),
                  pl.BlockSpec((l, FT_WIDTH), lambda i, n: (0, 2 * n + 1))],
        out_specs=pl.BlockSpec((tm, FT_WIDTH), lambda i, n: (i, n)),
        out_shape=jax.ShapeDtypeStruct((l, nb * FT_WIDTH), BF16),
        compiler_params=_params(("parallel", "arbitrary")),
        name="pdft",
    )(cl, snl, r, r)


def _dft_tables(l):
    w = FT_GROUP_W
    jk = (jnp.arange(w, dtype=I32)[:, None] * jnp.arange(w, dtype=I32)[None, :]) % w
    ang = jk.astype(F32) * (2.0 * math.pi / w)
    norm = 1.0 / math.sqrt(l * w)
    eye = jnp.eye(FT_GROUPS, dtype=F32)
    wc = jnp.concatenate([jnp.kron(eye, jnp.cos(ang) * norm), jnp.kron(eye, jnp.sin(ang) * norm)], axis=1)
    jl = (jnp.arange(l, dtype=I32)[:, None] * jnp.arange(l, dtype=I32)[None, :]) % l
    angl = jl.astype(F32) * (2.0 * math.pi / l)
    return wc.astype(BF16), jnp.cos(angl).astype(BF16), (-jnp.sin(angl)).astype(BF16)


def _merge_kernel(of_ref, ob_ref, zg_ref, zgh_ref, zgf_ref, yft_ref, x_ref, g1_ref, hgn_ref,
                  whg_ref, wft_ref, wout_ref, o_ref):
    o = of_ref[...] + ob_ref[...]
    hgn = hgn_ref[...]
    parts = []
    for h in range(HG_HEADS):
        oh = o[:, h * HG_DK:(h + 1) * HG_DK]
        ms = jnp.mean(oh * oh, axis=-1, keepdims=True)
        parts.append(oh * lax.rsqrt(ms + EPS) * hgn[:, h * HG_DK:(h + 1) * HG_DK])
    on = jnp.concatenate(parts, axis=1) * _silu(zg_ref[...])
    y_hg = _dot(on.astype(BF16), whg_ref[...])
    y_ft = _dot(yft_ref[...], wft_ref[...])
    y = _sigmoid(zgh_ref[...]) * y_hg + _sigmoid(zgf_ref[...]) * y_ft
    o_ref[...] = x_ref[...] + g1_ref[...] * _dot(y.astype(BF16), wout_ref[...])


def _merge(o_f, o_b, z, yft, x, g1, hgn, whg, wft, wout, tm):
    bsz, l, d = x.shape
    full = lambda shape: pl.BlockSpec(shape, lambda b, i: (0,) * len(shape))
    return pl.pallas_call(
        _merge_kernel,
        grid=(bsz, l // tm),
        in_specs=[pl.BlockSpec((None, tm, HG_WIDTH), lambda b, i: (b, i, 0)),
                  pl.BlockSpec((None, tm, HG_WIDTH), lambda b, i: (b, i, 0)),
                  pl.BlockSpec((None, tm, HG_WIDTH), lambda b, i: (b, i, 4)),
                  pl.BlockSpec((None, tm, d), lambda b, i: (b, i, 3)),
                  pl.BlockSpec((None, tm, d), lambda b, i: (b, i, 4)),
                  pl.BlockSpec((tm, FT_WIDTH), lambda b, i: (i, b)),
                  pl.BlockSpec((None, tm, d), lambda b, i: (b, i, 0)),
                  pl.BlockSpec((None, 1, d), lambda b, i: (b, 0, 0)),
                  full((1, HG_WIDTH)), full((HG_WIDTH, d)), full((FT_WIDTH, d)), full((d, d))],
        out_specs=pl.BlockSpec((None, tm, d), lambda b, i: (b, i, 0)),
        out_shape=jax.ShapeDtypeStruct((bsz, l, d), F32),
        compiler_params=_params(("parallel", "parallel")),
        name="merge",
    )(o_f, o_b, z, z, z, yft, x, g1, hgn, whg, wft, wout)


def _topk_rows(s, k, payload=None):
    n, t = s.shape
    iota = lax.broadcasted_iota(I32, (n, t), 0)
    sub = lax.broadcasted_iota(I32, (SUBLANES, t), 0)
    vals, sel = [], []
    for _ in range(k):
        v = [s[SUBLANES * r:SUBLANES * (r + 1)] for r in range(n // SUBLANES)]
        g = list(range(n // SUBLANES))
        while len(v) > 1:
            nv, ng = [], []
            for a in range(0, len(v) - 1, 2):
                nv.append(jnp.maximum(v[a], v[a + 1]))
                ng.append(jnp.where(v[a] >= v[a + 1], g[a], g[a + 1]))
            if len(v) % 2:
                nv.append(v[-1])
                ng.append(g[-1])
            v, g = nv, ng
        m = jnp.max(v[0], axis=0, keepdims=True)
        am = jnp.min(jnp.where(v[0] == m, g[0] * SUBLANES + sub, n), axis=0, keepdims=True)
        hit = iota == am
        vals.append(m)
        sel.append(am if payload is None else jnp.max(jnp.where(hit, payload, -1), axis=0, keepdims=True))
        s = jnp.where(hit, -jnp.inf, s)
    return jnp.concatenate(vals, axis=0), jnp.concatenate(sel, axis=0)


def _staircase():
    return [(a, PEER_TOPK // (a + 1)) for a in range(PEER_TOPK)]


def _route_kernel(h_ref, sh_ref, sc_ref, g_ref, wq_ref, keys_ref, u_ref, e_ref, gate_ref):
    u = _modulated_norm(h_ref[...], g_ref[...], sh_ref[...], sc_ref[...])
    u_ref[...] = u
    q = _dot(u.astype(BF16), wq_ref[...]).astype(BF16)
    k_ = PEER_TOPK
    e_rows, g_rows = [], []
    for h in range(PEER_HEADS):
        sv, si = [], []
        for p in range(2):
            hp = 2 * h + p
            s_t = _dot_nt(keys_ref[hp], q[:, hp * LANES:(hp + 1) * LANES])
            v, i = _topk_rows(s_t, k_)
            sv.append(v)
            si.append(i)
        stairs = _staircase()
        pad = -sum(nb for _, nb in stairs) % SUBLANES
        tm = q.shape[0]
        cand = jnp.concatenate([sv[0][a:a + 1] + sv[1][:nb] for a, nb in stairs]
                               + [jnp.full((pad, tm), -jnp.inf, F32)], axis=0)
        cidx = jnp.concatenate([si[0][a:a + 1] * PEER_NKEYS + si[1][:nb] for a, nb in stairs]
                               + [jnp.zeros((pad, tm), I32)], axis=0)
        cv, ce = _topk_rows(cand, k_, payload=cidx)
        ex = jnp.exp(cv - cv[0:1])
        g_rows.append(ex / jnp.sum(ex, axis=0, keepdims=True))
        e_rows.append(ce)
    e_ref[...] = (jnp.concatenate(e_rows, axis=0) * ROW_WORDS).T
    gate_ref[...] = jnp.concatenate(g_rows, axis=0).T


def _route(h1, sh, sc, g, wq, keys, tm):
    bsz, l, d = h1.shape
    full = lambda shape: pl.BlockSpec(shape, lambda b, i: (0,) * len(shape))
    return pl.pallas_call(
        _route_kernel,
        grid=(bsz, l // tm),
        in_specs=[pl.BlockSpec((None, tm, d), lambda b, i: (b, i, 0)),
                  pl.BlockSpec((None, 1, d), lambda b, i: (b, 0, 0)),
                  pl.BlockSpec((None, 1, d), lambda b, i: (b, 0, 0)),
                  full((1, d)), full(wq.shape), full(keys.shape)],
        out_specs=[pl.BlockSpec((None, tm, d), lambda b, i: (b, i, 0)),
                   pl.BlockSpec((None, tm, PEER_SLOTS), lambda b, i: (b, i, 0)),
                   pl.BlockSpec((None, tm, PEER_SLOTS), lambda b, i: (b, i, 0))],
        out_shape=[jax.ShapeDtypeStruct((bsz, l, d), F32),
                   jax.ShapeDtypeStruct((bsz, l, PEER_SLOTS), I32),
                   jax.ShapeDtypeStruct((bsz, l, PEER_SLOTS), F32)],
        compiler_params=_params(("parallel", "parallel")),
        name="route",
    )(h1, sh, sc, g, wq, keys)


def _pack_table(tab):
    e, d = tab.shape
    half = d // 2
    bits = lax.bitcast_convert_type(tab.astype(BF16), jnp.uint16).astype(jnp.uint32)
    word = bits[:, :half] | (bits[:, half:] << 16)
    return lax.bitcast_convert_type(word, I32).reshape(e * ROW_WORDS, LANES)


def _gather_rows(e_ref, t, tab_ref, g_ref):
    for j in range(PEER_SLOTS):
        first = pl.multiple_of(e_ref[t, j], ROW_WORDS)
        g_ref[j * ROW_WORDS:(j + 1) * ROW_WORDS, :] = tab_ref[pl.ds(first, ROW_WORDS), :]


def _slot_matrix(g_ref, s):
    return pltpu.bitcast(g_ref[pl.ds(s, PEER_SLOTS, stride=ROW_WORDS), :], BF16)


def _token_pipeline(tb, gather, compute, bufs):
    grp = len(bufs)
    last = tb - 1
    for k in range(grp):
        gather(k, bufs[k])

    def stage(i, carry):
        t = grp * i
        for k in range(grp):
            compute(t + k, bufs[k])
            gather(jnp.minimum(t + grp + k, last), bufs[k])
        return carry

    lax.fori_loop(0, tb // grp, stage, 0)


def _act_kernel(e_ref, gate_ref, x_ref, tab_ref, w_ref, p_ref, *bufs, tb):
    row = lax.broadcasted_iota(I32, (2 * SUBLANES, 2 * PEER_SLOTS), 0)
    col = lax.broadcasted_iota(I32, (2 * SUBLANES, 2 * PEER_SLOTS), 1)
    half_ok = ((row >> 2) & 1) == (col & 1)

    def compute(t, g_ref):
        xt = x_ref[t]
        x_hi = xt.astype(BF16).astype(F32)
        lhs = jnp.concatenate([x_hi, xt - x_hi], axis=0).astype(BF16)
        acc = jnp.zeros((2 * SUBLANES, 2 * PEER_SLOTS), F32)
        for s in range(ROW_WORDS):
            prod = _dot_nt(lhs, _slot_matrix(g_ref, s))
            acc = acc + jnp.where(half_ok & ((row & (ROW_WORDS - 1)) == s), prod, 0.0)
        p_ref[pl.ds(t, 1), :] = jnp.sum(acc, axis=0, keepdims=True)

    _token_pipeline(tb, lambda t, g: _gather_rows(e_ref, t, tab_ref, g), compute, bufs)
    pr = lax.broadcasted_iota(I32, (2 * PEER_SLOTS, PEER_SLOTS), 0)
    pc = lax.broadcasted_iota(I32, (2 * PEER_SLOTS, PEER_SLOTS), 1)
    pair = ((pr >> 1) == pc).astype(BF16)
    p_hi, p_lo = _split_bf16(p_ref[...])
    act = _dot(p_hi, pair) + _dot(p_lo, pair)
    w_ref[...] = gate_ref[...] * (0.5 * act * (1.0 + lax.erf(act * (2.0 ** -0.5))))


def _act(eidx, gates, x8, tab, tb):
    n = eidx.shape[0]
    return pl.pallas_call(
        functools.partial(_act_kernel, tb=tb),
        grid=(n // tb,),
        in_specs=[pl.BlockSpec((tb, PEER_SLOTS), lambda i: (i, 0), memory_space=pltpu.SMEM),
                  pl.BlockSpec((tb, PEER_SLOTS), lambda i: (i, 0)),
                  pl.BlockSpec((tb, SUBLANES, LANES), lambda i: (i, 0, 0)),
                  pl.BlockSpec(tab.shape, lambda i: (0, 0), pipeline_mode=pl.Buffered(1))],
        out_specs=pl.BlockSpec((tb, PEER_SLOTS), lambda i: (i, 0)),
        out_shape=jax.ShapeDtypeStruct((n, PEER_SLOTS), F32),
        scratch_shapes=[pltpu.VMEM((tb, 2 * PEER_SLOTS), F32)]
        + [pltpu.VMEM((PEER_SLOTS * ROW_WORDS, LANES), I32)] * GATHER_BUFS,
        compiler_params=_params(("arbitrary",)),
        name="peer_act",
    )(eidx, gates, x8, tab)


def _out_kernel(e_ref, w_ref, h_ref, g2_ref, fg_ref, tab_ref, o_ref, whi_ref, wlo_ref, *bufs, tb):
    er = lax.broadcasted_iota(I32, (PEER_SLOTS, 4 * PEER_SLOTS), 0)
    ec = lax.broadcasted_iota(I32, (PEER_SLOTS, 4 * PEER_SLOTS), 1)
    spread = (ec == jnp.where(ec < 2 * PEER_SLOTS, 2 * er, 2 * PEER_SLOTS + 2 * er + 1)).astype(BF16)
    w_hi, w_lo = _split_bf16(w_ref[...])
    whi_ref[...] = _dot(w_hi, spread)
    wlo_ref[...] = _dot(w_lo, spread)
    row = lax.broadcasted_iota(I32, (2 * SUBLANES, 2 * PEER_SLOTS), 0)

    def compute(t, g_ref):
        a = whi_ref[pl.ds(t, 1), :]
        b = wlo_ref[pl.ds(t, 1), :]
        half = 2 * PEER_SLOTS
        wrows = jnp.where(row < ROW_WORDS, a[:, :half], jnp.where(row < SUBLANES, a[:, half:],
                          jnp.where(row < SUBLANES + ROW_WORDS, b[:, :half], b[:, half:])))
        acc = jnp.zeros((2 * SUBLANES, LANES), F32)
        for s in range(ROW_WORDS):
            lhs = jnp.where((row & (ROW_WORDS - 1)) == s, wrows, 0.0).astype(BF16)
            acc = acc + _dot(lhs, _slot_matrix(g_ref, s))
        o_ref[t] = acc[:SUBLANES] + acc[SUBLANES:]

    _token_pipeline(tb, lambda t, g: _gather_rows(e_ref, t, tab_ref, g), compute, bufs)
    h2 = h_ref[...] + g2_ref[...] * o_ref[...]
    ms = jnp.sum(jnp.sum(h2 * h2, axis=2, keepdims=True), axis=1, keepdims=True) * (1.0 / (SUBLANES * LANES))
    o_ref[...] = h2 * lax.rsqrt(ms + EPS) * fg_ref[...]


def _peer_out(eidx, w, h8, g2_8, fg8, tab, tb, tokens_per_batch):
    n = eidx.shape[0]
    return pl.pallas_call(
        functools.partial(_out_kernel, tb=tb),
        grid=(n // tb,),
        in_specs=[pl.BlockSpec((tb, PEER_SLOTS), lambda i: (i, 0), memory_space=pltpu.SMEM),
                  pl.BlockSpec((tb, PEER_SLOTS), lambda i: (i, 0)),
                  pl.BlockSpec((tb, SUBLANES, LANES), lambda i: (i, 0, 0)),
                  pl.BlockSpec((None, SUBLANES, LANES), lambda i: ((i * tb) // tokens_per_batch, 0, 0)),
                  pl.BlockSpec((SUBLANES, LANES), lambda i: (0, 0)),
                  pl.BlockSpec(tab.shape, lambda i: (0, 0), pipeline_mode=pl.Buffered(1))],
        out_specs=pl.BlockSpec((tb, SUBLANES, LANES), lambda i: (i, 0, 0)),
        out_shape=jax.ShapeDtypeStruct((n, SUBLANES, LANES), F32),
        scratch_shapes=[pltpu.VMEM((tb, 4 * PEER_SLOTS), F32)] * 2
        + [pltpu.VMEM((PEER_SLOTS * ROW_WORDS, LANES), I32)] * GATHER_BUFS,
        compiler_params=_params(("arbitrary",)),
        name="peer_out",
    )(eidx, w, h8, g2_8, fg8, tab)


def kernel(x, c, ctx, c_ctx, w_ada, b_ada, norm_mix_g, norm_ffn_g, w_in, hg_lb_f, hg_lb_b, hg_norm_g,
           w_hg_out, w_ft_out, w_out, peer_w_q, peer_sub_keys, peer_u, peer_v, final_norm_g):
    bsz, l, d = x.shape
    n = bsz * l
    layer = 0

    lb_f = jnp.cumsum(jax.nn.softmax(hg_lb_f.astype(F32), axis=0), axis=0)[layer][None, :]
    lb_b = jnp.cumsum(jax.nn.softmax(hg_lb_b.astype(F32), axis=0), axis=0)[layer][None, :]

    rows = -(-(bsz + 1) // SUBLANES) * SUBLANES
    cc = jnp.zeros((rows, d), F32).at[:bsz].set(c).at[bsz].set(c_ctx)
    mod = _ada(cc, w_ada[layer], b_ada[layer][None, :])
    sh1, sc1, g1, sh2, sc2, g2 = [mod[:bsz, k * d:(k + 1) * d][:, None, :] for k in range(6)]
    sh1c, sc1c = [jnp.broadcast_to(mod[bsz, k * d:(k + 1) * d][None, None, :], (bsz, 1, d)) for k in range(2)]

    g_mix = norm_mix_g[layer][None, :]
    w_in_b = w_in[layer].astype(BF16)
    z = _normmm(x, sh1, sc1, g_mix, w_in_b, tm=256)
    zc = _normmm(ctx, sh1c, sc1c, g_mix, w_in_b[:, HG_WIDTH:4 * HG_WIDTH], tm=ctx.shape[1])

    o_f, o_b = _scan(z, zc, lb_f, lb_b, tl=512)

    wc, cl, snl = _dft_tables(l)
    yft = _pdft(cl, snl, _cdft(z, wc, tm=512), tm=512)

    h1 = _merge(o_f, o_b, z, yft, x, g1, hg_norm_g[layer].reshape(1, HG_WIDTH),
                w_hg_out[layer].astype(BF16), w_ft_out[layer].astype(BF16), w_out[layer].astype(BF16), tm=512)

    keys = peer_sub_keys[layer].reshape(PEER_HEADS * 2, PEER_NKEYS, -1).astype(BF16)
    u2, eidx, gates = _route(h1, sh2, sc2, norm_ffn_g[layer][None, :], peer_w_q[layer].astype(BF16), keys, tm=256)

    eidx = eidx.reshape(n, PEER_SLOTS)
    w = _act(eidx, gates.reshape(n, PEER_SLOTS), u2.reshape(n, SUBLANES, LANES), _pack_table(peer_u[layer]),
             tb=PEER_TB)
    out = _peer_out(eidx, w, h1.reshape(n, SUBLANES, LANES), g2.reshape(bsz, SUBLANES, LANES),
                    final_norm_g.reshape(SUBLANES, LANES), _pack_table(peer_v[layer]), tb=PEER_TB, tokens_per_batch=l)
    return out.reshape(bsz, l, d)
```

```python
import functools
import math

import jax
import jax.numpy as jnp
from jax import lax
from jax.experimental import pallas as pl
from jax.experimental.pallas import tpu as pltpu

F32 = jnp.float32
BF16 = jnp.bfloat16
I32 = jnp.int32

EPS = 1e-6
HG_HEADS = 4
HG_DK = 128
HG_WIDTH = HG_HEADS * HG_DK
CHUNK = 64
FT_GROUPS = 4
FT_GROUP_W = 128
FT_WIDTH = FT_GROUPS * FT_GROUP_W
PEER_HEADS = 8
PEER_NKEYS = 128
PEER_TOPK = 16
PEER_SLOTS = PEER_HEADS * PEER_TOPK
LANES = 128
SUBLANES = 8
ROW_WORDS = 4
SLOTS_PER_TILE = 32
PEER_TB = 32
VMEM_LIMIT = 56 * 1024 * 1024


def _sigmoid(x):
    return 1.0 / (1.0 + jnp.exp(-x))


def _silu(x):
    return x * _sigmoid(x)


def _dot(a, b):
    return jnp.dot(a, b, preferred_element_type=F32)


def _dot_nt(a, b):
    return lax.dot_general(a, b, (((1,), (1,)), ((), ())), preferred_element_type=F32)


def _split_bf16(x):
    hi = x.astype(BF16)
    lo = (x - hi.astype(F32)).astype(BF16)
    return hi, lo


def _params(sem, limit=VMEM_LIMIT):
    return pltpu.CompilerParams(dimension_semantics=sem, vmem_limit_bytes=limit)


def _ada_kernel(c_ref, w_ref, b_ref, o_ref):
    a = _silu(c_ref[...])
    a_hi, a_lo = _split_bf16(a)
    w_hi, w_lo = _split_bf16(w_ref[...])
    o_ref[...] = _dot(a_hi, w_hi) + _dot(a_lo, w_hi) + _dot(a_hi, w_lo) + b_ref[...]


def _ada(cc, w, b):
    rows, d = cc.shape
    cols = w.shape[1]
    tn = 1024
    return pl.pallas_call(
        _ada_kernel,
        grid=(cols // tn,),
        in_specs=[pl.BlockSpec((rows, d), lambda j: (0, 0)),
                  pl.BlockSpec((d, tn), lambda j: (0, j)),
                  pl.BlockSpec((1, tn), lambda j: (0, j))],
        out_specs=pl.BlockSpec((rows, tn), lambda j: (0, j)),
        out_shape=jax.ShapeDtypeStruct((rows, cols), F32),
        compiler_params=_params(("parallel",)),
        name="ada",
    )(cc, w, b)


def _modulated_norm(x, g, shift, scale):
    y = x * lax.rsqrt(jnp.mean(x * x, axis=-1, keepdims=True) + EPS)
    return y * g * (1.0 + scale) + shift


def _normmm_kernel(h_ref, sh_ref, sc_ref, g_ref, w_ref, o_ref, *, tn):
    u = _modulated_norm(h_ref[...], g_ref[...], sh_ref[...], sc_ref[...]).astype(BF16)
    for j in range(o_ref.shape[-1] // tn):
        o_ref[:, j * tn:(j + 1) * tn] = _dot(u, w_ref[:, j * tn:(j + 1) * tn])


def _normmm(h, shift, scale, g, w, tm):
    bsz, l, d = h.shape
    c = w.shape[1]
    return pl.pallas_call(
        functools.partial(_normmm_kernel, tn=512),
        grid=(bsz, l // tm),
        in_specs=[pl.BlockSpec((None, tm, d), lambda b, i: (b, i, 0)),
                  pl.BlockSpec((None, 1, d), lambda b, i: (b, 0, 0)),
                  pl.BlockSpec((None, 1, d), lambda b, i: (b, 0, 0)),
                  pl.BlockSpec((1, d), lambda b, i: (0, 0)),
                  pl.BlockSpec((d, c), lambda b, i: (0, 0))],
        out_specs=pl.BlockSpec((None, tm, c), lambda b, i: (b, i, 0)),
        out_shape=jax.ShapeDtypeStruct((bsz, l, c), F32),
        compiler_params=_params(("parallel", "parallel")),
        name="normmm",
    )(h, shift, scale, g, w)


def _scan_kernel(qf_ref, ff_ref, vf_ref, qb_ref, fb_ref, vb_ref, cff_ref, cfb_ref, cv_ref,
                 lbf_ref, lbb_ref, of_ref, ob_ref, sf_ref, sb_ref, *, tl, ctx_len):
    j = pl.program_id(2)
    c_ = CHUNK
    row = lax.broadcasted_iota(I32, (c_, c_), 0)
    col = lax.broadcasted_iota(I32, (c_, c_), 1)
    tril = col <= row
    triu = col >= row
    tril_b = tril.astype(BF16)
    triu_b = triu.astype(BF16)
    lbf = lbf_ref[...]
    lbb = lbb_ref[...]
    qscale = HG_DK ** -0.5

    def decays(fz, lb, fwd):
        f = lb + (1.0 - lb) * _sigmoid(fz)
        hi, lo = _split_bf16(jnp.log(f))
        tri = tril_b if fwd else triu_b
        b = _dot(tri, hi) + _dot(tri, lo)
        if fwd:
            return f, b, b[c_ // 2 - 1:c_ // 2], b[c_ - 1:c_]
        return f, b, b[c_ // 2:c_ // 2 + 1], b[0:1]

    def next_state(s_t, f, b, b_last, v):
        kd = (1.0 - f) * jnp.exp(b_last - b)
        return jnp.exp(b_last) * s_t + _dot(v.T.astype(BF16), kd.astype(BF16))

    def chunk(q, fz, v, lb, s_t, fwd):
        f, b, b_mid, b_last = decays(fz, lb, fwd)
        qs = q * qscale
        qt = qs * jnp.exp(b - b_mid)
        kt = (1.0 - f) * jnp.exp(b_mid - b)
        a = _dot_nt(qt.astype(BF16), kt.astype(BF16))
        a = jnp.where(tril if fwd else triu, a, 0.0)
        o = _dot(a.astype(BF16), v.astype(BF16))
        o = o + _dot_nt((qs * jnp.exp(b)).astype(BF16), s_t.astype(BF16))
        return o, next_state(s_t, f, b, b_last, v)

    @pl.when(j == 0)
    def _():
        s_f = jnp.zeros((HG_DK, HG_DK), F32)
        s_b = jnp.zeros((HG_DK, HG_DK), F32)
        n = ctx_len // c_
        for c in range(n):
            f, b, _, b_last = decays(cff_ref[c * c_:(c + 1) * c_, :], lbf, True)
            s_f = next_state(s_f, f, b, b_last, cv_ref[c * c_:(c + 1) * c_, :])
            cb = n - 1 - c
            f, b, _, b_last = decays(cfb_ref[cb * c_:(cb + 1) * c_, :], lbb, False)
            s_b = next_state(s_b, f, b, b_last, cv_ref[cb * c_:(cb + 1) * c_, :])
        sf_ref[...] = s_f
        sb_ref[...] = s_b

    n_chunks = tl // c_

    def step(c, carry):
        r = pl.multiple_of(c * c_, c_)
        o, s = chunk(qf_ref[pl.ds(r, c_), :], ff_ref[pl.ds(r, c_), :], vf_ref[pl.ds(r, c_), :],
                     lbf, sf_ref[...], True)
        of_ref[pl.ds(r, c_), :] = o
        sf_ref[...] = s
        rb = pl.multiple_of((n_chunks - 1 - c) * c_, c_)
        o, s = chunk(qb_ref[pl.ds(rb, c_), :], fb_ref[pl.ds(rb, c_), :], vb_ref[pl.ds(rb, c_), :],
                     lbb, sb_ref[...], False)
        ob_ref[pl.ds(rb, c_), :] = o
        sb_ref[...] = s
        return carry

    lax.fori_loop(0, n_chunks, step, 0)


def _scan(z, zc, lb_f, lb_b, tl):
    bsz, l, _ = z.shape
    ctx_len = zc.shape[1]
    nj = l // tl
    hb = HG_HEADS
    blk = lambda colgrp, rev: pl.BlockSpec(
        (None, tl, HG_DK),
        (lambda b, h, j: (b, nj - 1 - j, colgrp * hb + h)) if rev else (lambda b, h, j: (b, j, colgrp * hb + h)))
    cblk = lambda colgrp: pl.BlockSpec((None, ctx_len, HG_DK), lambda b, h, j: (b, 0, colgrp * hb + h))
    lbspec = pl.BlockSpec((1, HG_DK), lambda b, h, j: (0, h))
    ospec = lambda rev: pl.BlockSpec(
        (None, tl, HG_DK), (lambda b, h, j: (b, nj - 1 - j, h)) if rev else (lambda b, h, j: (b, j, h)))
    return pl.pallas_call(
        functools.partial(_scan_kernel, tl=tl, ctx_len=ctx_len),
        grid=(bsz, hb, nj),
        in_specs=[blk(0, False), blk(1, False), blk(3, False),
                  blk(0, True), blk(2, True), blk(3, True),
                  cblk(0), cblk(1), cblk(2), lbspec, lbspec],
        out_specs=[ospec(False), ospec(True)],
        out_shape=[jax.ShapeDtypeStruct((bsz, l, HG_WIDTH), F32)] * 2,
        scratch_shapes=[pltpu.VMEM((HG_DK, HG_DK), F32), pltpu.VMEM((HG_DK, HG_DK), F32)],
        compiler_params=_params(("parallel", "parallel", "arbitrary")),
        name="scan",
    )(z, z, z, z, z, z, zc, zc, zc, lb_f, lb_b)


def _cdft_kernel(x_ref, w_ref, o_ref):
    o_ref[...] = _dot(x_ref[...].astype(BF16), w_ref[...]).astype(BF16)


def _cdft(z, wc, tm):
    bsz, l, _ = z.shape
    ftblk = 5 * HG_WIDTH // FT_WIDTH
    return pl.pallas_call(
        _cdft_kernel,
        grid=(bsz, l // tm),
        in_specs=[pl.BlockSpec((None, tm, FT_WIDTH), lambda b, i: (b, i, ftblk)),
                  pl.BlockSpec((FT_WIDTH, 2 * FT_WIDTH), lambda b, i: (0, 0))],
        out_specs=pl.BlockSpec((tm, 2 * FT_WIDTH), lambda b, i: (i, b)),
        out_shape=jax.ShapeDtypeStruct((l, bsz * 2 * FT_WIDTH), BF16),
        compiler_params=_params(("parallel", "parallel")),
        name="cdft",
    )(z, wc)


def _pdft_kernel(cl_ref, sl_ref, xc_ref, xs_ref, o_ref):
    o_ref[...] = (_dot(cl_ref[...], xc_ref[...]) + _dot(sl_ref[...], xs_ref[...])).astype(BF16)


def _pdft(cl, snl, r, tm):
    l = cl.shape[0]
    nb = r.shape[1] // (2 * FT_WIDTH)
    return pl.pallas_call(
        _pdft_kernel,
        grid=(l // tm, nb),
        in_specs=[pl.BlockSpec((tm, l), lambda i, n: (i, 0)),
                  pl.BlockSpec((tm, l), lambda i, n: (i, 0)),
                  pl.BlockSpec((l, FT_WIDTH), lambda i, n: (0, 2 * n)),
                  pl.BlockSpec((l, FT_WIDTH), lambda i, n: (0, 2 * n + 1))],
        out_specs=pl.BlockSpec((tm, FT_WIDTH), lambda i, n: (i, n)),
        out_shape=jax.ShapeDtypeStruct((l, nb * FT_WIDTH), BF16),
        compiler_params=_params(("parallel", "arbitrary")),
        name="pdft",
    )(cl, snl, r, r)


def _dft_tables(l):
    w = FT_GROUP_W
    jk = (jnp.arange(w, dtype=I32)[:, None] * jnp.arange(w, dtype=I32)[None, :]) % w
    ang = jk.astype(F32) * (2.0 * math.pi / w)
    norm = 1.0 / math.sqrt(l * w)
    eye = jnp.eye(FT_GROUPS, dtype=F32)
    wc = jnp.concatenate([jnp.kron(eye, jnp.cos(ang) * norm), jnp.kron(eye, jnp.sin(ang) * norm)], axis=1)
    jl = (jnp.arange(l, dtype=I32)[:, None] * jnp.arange(l, dtype=I32)[None, :]) % l
    angl = jl.astype(F32) * (2.0 * math.pi / l)
    return wc.astype(BF16), jnp.cos(angl).astype(BF16), (-jnp.sin(angl)).astype(BF16)


def _merge_kernel(of_ref, ob_ref, zg_ref, zgh_ref, zgf_ref, yft_ref, x_ref, g1_ref, hgn_ref,
                  whg_ref, wft_ref, wout_ref, o_ref):
    o = of_ref[...] + ob_ref[...]
    hgn = hgn_ref[...]
    parts = []
    for h in range(HG_HEADS):
        oh = o[:, h * HG_DK:(h + 1) * HG_DK]
        ms = jnp.mean(oh * oh, axis=-1, keepdims=True)
        parts.append(oh * lax.rsqrt(ms + EPS) * hgn[:, h * HG_DK:(h + 1) * HG_DK])
    on = jnp.concatenate(parts, axis=1) * _silu(zg_ref[...])
    y_hg = _dot(on.astype(BF16), whg_ref[...])
    y_ft = _dot(yft_ref[...], wft_ref[...])
    y = _sigmoid(zgh_ref[...]) * y_hg + _sigmoid(zgf_ref[...]) * y_ft
    o_ref[...] = x_ref[...] + g1_ref[...] * _dot(y.astype(BF16), wout_ref[...])


def _merge(o_f, o_b, z, yft, x, g1, hgn, whg, wft, wout, tm):
    bsz, l, d = x.shape
    full = lambda shape: pl.BlockSpec(shape, lambda b, i: (0,) * len(shape))
    return pl.pallas_call(
        _merge_kernel,
        grid=(bsz, l // tm),
        in_specs=[pl.BlockSpec((None, tm, HG_WIDTH), lambda b, i: (b, i, 0)),
                  pl.BlockSpec((None, tm, HG_WIDTH), lambda b, i: (b, i, 0)),
                  pl.BlockSpec((None, tm, HG_WIDTH), lambda b, i: (b, i, 4)),
                  pl.BlockSpec((None, tm, d), lambda b, i: (b, i, 3)),
                  pl.BlockSpec((None, tm, d), lambda b, i: (b, i, 4)),
                  pl.BlockSpec((tm, FT_WIDTH), lambda b, i: (i, b)),
                  pl.BlockSpec((None, tm, d), lambda b, i: (b, i, 0)),
                  pl.BlockSpec((None, 1, d), lambda b, i: (b, 0, 0)),
                  full((1, HG_WIDTH)), full((HG_WIDTH, d)), full((FT_WIDTH, d)), full((d, d))],
        out_specs=pl.BlockSpec((None, tm, d), lambda b, i: (b, i, 0)),
        out_shape=jax.ShapeDtypeStruct((bsz, l, d), F32),
        compiler_params=_params(("parallel", "parallel")),
        name="merge",
    )(o_f, o_b, z, z, z, yft, x, g1, hgn, whg, wft, wout)


def _topk_rows(s, k, payload=None):
    n, t = s.shape
    iota = lax.broadcasted_iota(I32, (n, t), 0)
    sub = lax.broadcasted_iota(I32, (SUBLANES, t), 0)
    vals, sel = [], []
    for _ in range(k):
        v = [s[SUBLANES * r:SUBLANES * (r + 1)] for r in range(n // SUBLANES)]
        g = list(range(n // SUBLANES))
        while len(v) > 1:
            nv, ng = [], []
            for a in range(0, len(v) - 1, 2):
                nv.append(jnp.maximum(v[a], v[a + 1]))
                ng.append(jnp.where(v[a] >= v[a + 1], g[a], g[a + 1]))
            if len(v) % 2:
                nv.append(v[-1])
                ng.append(g[-1])
            v, g = nv, ng
        m = jnp.max(v[0], axis=0, keepdims=True)
        am = jnp.min(jnp.where(v[0] == m, g[0] * SUBLANES + sub, n), axis=0, keepdims=True)
        hit = iota == am
        vals.append(m)
        sel.append(am if payload is None else jnp.max(jnp.where(hit, payload, -1), axis=0, keepdims=True))
        s = jnp.where(hit, -jnp.inf, s)
    return jnp.concatenate(vals, axis=0), jnp.concatenate(sel, axis=0)


def _staircase():
    return [(a, PEER_TOPK // (a + 1)) for a in range(PEER_TOPK)]


def _route_kernel(h_ref, sh_ref, sc_ref, g_ref, wq_ref, keys_ref, u_ref, e_ref, gate_ref):
    u = _modulated_norm(h_ref[...], g_ref[...], sh_ref[...], sc_ref[...])
    u_ref[...] = u
    q = _dot(u.astype(BF16), wq_ref[...]).astype(BF16)
    k_ = PEER_TOPK
    e_rows, g_rows = [], []
    for h in range(PEER_HEADS):
        sv, si = [], []
        for p in range(2):
            hp = 2 * h + p
            s_t = _dot_nt(keys_ref[hp], q[:, hp * LANES:(hp + 1) * LANES])
            v, i = _topk_rows(s_t, k_)
            sv.append(v)
            si.append(i)
        stairs = _staircase()
        pad = -sum(nb for _, nb in stairs) % SUBLANES
        tm = q.shape[0]
        cand = jnp.concatenate([sv[0][a:a + 1] + sv[1][:nb] for a, nb in stairs]
                               + [jnp.full((pad, tm), -jnp.inf, F32)], axis=0)
        cidx = jnp.concatenate([si[0][a:a + 1] * PEER_NKEYS + si[1][:nb] for a, nb in stairs]
                               + [jnp.zeros((pad, tm), I32)], axis=0)
        cv, ce = _topk_rows(cand, k_, payload=cidx)
        ex = jnp.exp(cv - cv[0:1])
        g_rows.append(ex / jnp.sum(ex, axis=0, keepdims=True))
        e_rows.append(ce)
    e_ref[...] = (jnp.concatenate(e_rows, axis=0) * ROW_WORDS).T
    gate_ref[...] = jnp.concatenate(g_rows, axis=0).T


def _route(h1, sh, sc, g, wq, keys, tm):
    bsz, l, d = h1.shape
    full = lambda shape: pl.BlockSpec(shape, lambda b, i: (0,) * len(shape))
    return pl.pallas_call(
        _route_kernel,
        grid=(bsz, l // tm),
        in_specs=[pl.BlockSpec((None, tm, d), lambda b, i: (b, i, 0)),
                  pl.BlockSpec((None, 1, d), lambda b, i: (b, 0, 0)),
                  pl.BlockSpec((None, 1, d), lambda b, i: (b, 0, 0)),
                  full((1, d)), full(wq.shape), full(keys.shape)],
        out_specs=[pl.BlockSpec((None, tm, d), lambda b, i: (b, i, 0)),
                   pl.BlockSpec((None, tm, PEER_SLOTS), lambda b, i: (b, i, 0)),
                   pl.BlockSpec((None, tm, PEER_SLOTS), lambda b, i: (b, i, 0))],
        out_shape=[jax.ShapeDtypeStruct((bsz, l, d), F32),
                   jax.ShapeDtypeStruct((bsz, l, PEER_SLOTS), I32),
                   jax.ShapeDtypeStruct((bsz, l, PEER_SLOTS), F32)],
        compiler_params=_params(("parallel", "parallel")),
        name="route",
    )(h1, sh, sc, g, wq, keys)


def _pack_table(tab):
    e, d = tab.shape
    half = d // 2
    bits = lax.bitcast_convert_type(tab.astype(BF16), jnp.uint16).astype(jnp.uint32)
    word = bits[:, :half] | (bits[:, half:] << 16)
    return lax.bitcast_convert_type(word, I32).reshape(e * ROW_WORDS, LANES)


def _gathered_tile(e_ref, t, tab_ref, kt):
    rows = []
    for j in range(kt * SLOTS_PER_TILE, (kt + 1) * SLOTS_PER_TILE):
        first = pl.multiple_of(e_ref[t, j], ROW_WORDS)
        rows.append(tab_ref[pl.ds(first, ROW_WORDS), :])
    return pltpu.bitcast(jnp.concatenate(rows, axis=0), BF16)


def _element_row_code(r):
    return 2 * (r & (ROW_WORDS - 1)) + ((r >> 2) & 1)


def _index_scratch(tb):
    return [pltpu.SMEM((tb, PEER_SLOTS), I32)] * 2 + [pltpu.SemaphoreType.DMA((2,))]


def _for_each_token(e_hbm, e_bufs, sems, tb, body):
    i = pl.program_id(0)
    n = pl.num_programs(0)

    def copy(step, slot):
        return pltpu.make_async_copy(e_hbm.at[pl.ds(step * tb, tb)], e_bufs[slot], sems.at[slot])

    @pl.when(i == 0)
    def _():
        copy(0, 0).start()

    for slot in range(2):
        @pl.when((i & 1) == slot)
        def _(slot=slot):
            copy(i, slot).wait()

            @pl.when(i + 1 < n)
            def _():
                copy(i + 1, 1 - slot).start()

            for t in range(tb):
                body(e_bufs[slot], t)


def _act_kernel(e_hbm, gate_ref, x_ref, tab_ref, w_ref, p_ref, e_a, e_b, sems, *, tb):
    tile_cols = SUBLANES * SLOTS_PER_TILE
    row = lax.broadcasted_iota(I32, (2 * SUBLANES, tile_cols), 0)
    col = lax.broadcasted_iota(I32, (2 * SUBLANES, tile_cols), 1)
    wanted = _element_row_code(row) == (col & (SUBLANES - 1))

    def token(e_ref, t):
        xt = x_ref[t]
        x_hi = xt.astype(BF16).astype(F32)
        lhs = jnp.concatenate([x_hi, xt - x_hi], axis=0).astype(BF16)
        for kt in range(PEER_SLOTS // SLOTS_PER_TILE):
            prod = _dot_nt(lhs, _gathered_tile(e_ref, t, tab_ref, kt))
            p_ref[t:t + 1, kt * tile_cols:(kt + 1) * tile_cols] = jnp.sum(
                jnp.where(wanted, prod, 0.0), axis=0, keepdims=True)

    _for_each_token(e_hbm, (e_a, e_b), sems, tb, token)
    pr = lax.broadcasted_iota(I32, (SUBLANES * PEER_SLOTS, PEER_SLOTS), 0)
    pc = lax.broadcasted_iota(I32, (SUBLANES * PEER_SLOTS, PEER_SLOTS), 1)
    fold = ((pr >> 3) == pc).astype(BF16)
    p_hi, p_lo = _split_bf16(p_ref[...])
    act = _dot(p_hi, fold) + _dot(p_lo, fold)
    w_ref[...] = gate_ref[...] * (0.5 * act * (1.0 + lax.erf(act * (2.0 ** -0.5))))


def _act(eidx, gates, x8, tab, tb):
    n = eidx.shape[0]
    return pl.pallas_call(
        functools.partial(_act_kernel, tb=tb),
        grid=(n // tb,),
        in_specs=[pl.BlockSpec(memory_space=pl.ANY),
                  pl.BlockSpec((tb, PEER_SLOTS), lambda i: (i, 0)),
                  pl.BlockSpec((tb, SUBLANES, LANES), lambda i: (i, 0, 0)),
                  pl.BlockSpec(tab.shape, lambda i: (0, 0), pipeline_mode=pl.Buffered(1))],
        out_specs=pl.BlockSpec((tb, PEER_SLOTS), lambda i: (i, 0)),
        out_shape=jax.ShapeDtypeStruct((n, PEER_SLOTS), F32),
        scratch_shapes=[pltpu.VMEM((tb, SUBLANES * PEER_SLOTS), F32)] + _index_scratch(tb),
        compiler_params=_params(("arbitrary",)),
        name="peer_act",
    )(eidx, gates, x8, tab)


def _out_kernel(e_hbm, w_ref, h_ref, g2_ref, fg_ref, tab_ref, o_ref, whi_ref, wlo_ref, e_a, e_b, sems, *, tb):
    er = lax.broadcasted_iota(I32, (PEER_SLOTS, SUBLANES * PEER_SLOTS), 0)
    ec = lax.broadcasted_iota(I32, (PEER_SLOTS, SUBLANES * PEER_SLOTS), 1)
    spread = ((ec >> 3) == er).astype(BF16)
    w_hi, w_lo = _split_bf16(w_ref[...])
    whi_ref[...] = _dot(w_hi, spread)
    wlo_ref[...] = _dot(w_lo, spread)
    tile_cols = SUBLANES * SLOTS_PER_TILE
    row = lax.broadcasted_iota(I32, (2 * SUBLANES, tile_cols), 0)
    col = lax.broadcasted_iota(I32, (2 * SUBLANES, tile_cols), 1)
    wanted = _element_row_code(row) == (col & (SUBLANES - 1))

    def token(e_ref, t):
        acc = jnp.zeros((2 * SUBLANES, LANES), F32)
        for kt in range(PEER_SLOTS // SLOTS_PER_TILE):
            a = whi_ref[t:t + 1, kt * tile_cols:(kt + 1) * tile_cols]
            b = wlo_ref[t:t + 1, kt * tile_cols:(kt + 1) * tile_cols]
            lhs = jnp.where(wanted, jnp.where(row < SUBLANES, a, b), 0.0).astype(BF16)
            acc = acc + _dot(lhs, _gathered_tile(e_ref, t, tab_ref, kt))
        o_ref[t] = acc[:SUBLANES] + acc[SUBLANES:]

    _for_each_token(e_hbm, (e_a, e_b), sems, tb, token)
    h2 = h_ref[...] + g2_ref[...] * o_ref[...]
    ms = jnp.sum(jnp.sum(h2 * h2, axis=2, keepdims=True), axis=1, keepdims=True) * (1.0 / (SUBLANES * LANES))
    o_ref[...] = h2 * lax.rsqrt(ms + EPS) * fg_ref[...]


def _peer_out(eidx, w, h8, g2_8, fg8, tab, tb, tokens_per_batch):
    n = eidx.shape[0]
    return pl.pallas_call(
        functools.partial(_out_kernel, tb=tb),
        grid=(n // tb,),
        in_specs=[pl.BlockSpec(memory_space=pl.ANY),
                  pl.BlockSpec((tb, PEER_SLOTS), lambda i: (i, 0)),
                  pl.BlockSpec((tb, SUBLANES, LANES), lambda i: (i, 0, 0)),
                  pl.BlockSpec((None, SUBLANES, LANES), lambda i: ((i * tb) // tokens_per_batch, 0, 0)),
                  pl.BlockSpec((SUBLANES, LANES), lambda i: (0, 0)),
                  pl.BlockSpec(tab.shape, lambda i: (0, 0), pipeline_mode=pl.Buffered(1))],
        out_specs=pl.BlockSpec((tb, SUBLANES, LANES), lambda i: (i, 0, 0)),
        out_shape=jax.ShapeDtypeStruct((n, SUBLANES, LANES), F32),
        scratch_shapes=[pltpu.VMEM((tb, SUBLANES * PEER_SLOTS), F32)] * 2 + _index_scratch(tb),
        compiler_params=_params(("arbitrary",)),
        name="peer_out",
    )(eidx, w, h8, g2_8, fg8, tab)


def kernel(x, c, ctx, c_ctx, w_ada, b_ada, norm_mix_g, norm_ffn_g, w_in, hg_lb_f, hg_lb_b, hg_norm_g,
           w_hg_out, w_ft_out, w_out, peer_w_q, peer_sub_keys, peer_u, peer_v, final_norm_g):
    bsz, l, d = x.shape
    n = bsz * l
    layer = 0

    lb_f = jnp.cumsum(jax.nn.softmax(hg_lb_f.astype(F32), axis=0), axis=0)[layer][None, :]
    lb_b = jnp.cumsum(jax.nn.softmax(hg_lb_b.astype(F32), axis=0), axis=0)[layer][None, :]

    rows = -(-(bsz + 1) // SUBLANES) * SUBLANES
    cc = jnp.zeros((rows, d), F32).at[:bsz].set(c).at[bsz].set(c_ctx)
    mod = _ada(cc, w_ada[layer], b_ada[layer][None, :])
    sh1, sc1, g1, sh2, sc2, g2 = [mod[:bsz, k * d:(k + 1) * d][:, None, :] for k in range(6)]
    sh1c, sc1c = [jnp.broadcast_to(mod[bsz, k * d:(k + 1) * d][None, None, :], (bsz, 1, d)) for k in range(2)]

    g_mix = norm_mix_g[layer][None, :]
    w_in_b = w_in[layer].astype(BF16)
    z = _normmm(x, sh1, sc1, g_mix, w_in_b, tm=256)
    zc = _normmm(ctx, sh1c, sc1c, g_mix, w_in_b[:, HG_WIDTH:4 * HG_WIDTH], tm=ctx.shape[1])

    o_f, o_b = _scan(z, zc, lb_f, lb_b, tl=512)

    wc, cl, snl = _dft_tables(l)
    yft = _pdft(cl, snl, _cdft(z, wc, tm=512), tm=512)

    h1 = _merge(o_f, o_b, z, yft, x, g1, hg_norm_g[layer].reshape(1, HG_WIDTH),
                w_hg_out[layer].astype(BF16), w_ft_out[layer].astype(BF16), w_out[layer].astype(BF16), tm=512)

    keys = peer_sub_keys[layer].reshape(PEER_HEADS * 2, PEER_NKEYS, -1).astype(BF16)
    u2, eidx, gates = _route(h1, sh2, sc2, norm_ffn_g[layer][None, :], peer_w_q[layer].astype(BF16), keys, tm=256)

    eidx = eidx.reshape(n, PEER_SLOTS)
    w = _act(eidx, gates.reshape(n, PEER_SLOTS), u2.reshape(n, SUBLANES, LANES), _pack_table(peer_u[layer]),
             tb=PEER_TB)
    out = _peer_out(eidx, w, h1.reshape(n, SUBLANES, LANES), g2.reshape(bsz, SUBLANES, LANES),
                    final_norm_g.reshape(SUBLANES, LANES), _pack_table(peer_v[layer]), tb=PEER_TB, tokens_per_batch=l)
    return out.reshape(bsz, l, d)
```

```python
import functools
import math

import jax
import jax.numpy as jnp
from jax import lax
from jax.experimental import pallas as pl
from jax.experimental.pallas import tpu as pltpu

F32 = jnp.float32
BF16 = jnp.bfloat16
I32 = jnp.int32

EPS = 1e-6
HG_HEADS = 4
HG_DK = 128
HG_WIDTH = HG_HEADS * HG_DK
CHUNK = 64
FT_GROUPS = 4
FT_GROUP_W = 128
FT_WIDTH = FT_GROUPS * FT_GROUP_W
PEER_HEADS = 8
PEER_NKEYS = 128
PEER_TOPK = 16
PEER_SLOTS = PEER_HEADS * PEER_TOPK
LANES = 128
SUBLANES = 8
ROW_WORDS = 4
SLOTS_PER_TILE = 32
PEER_TB = 64
VMEM_LIMIT = 56 * 1024 * 1024


def _sigmoid(x):
    return 1.0 / (1.0 + jnp.exp(-x))


def _silu(x):
    return x * _sigmoid(x)


def _dot(a, b):
    return jnp.dot(a, b, preferred_element_type=F32)


def _dot_nt(a, b):
    return lax.dot_general(a, b, (((1,), (1,)), ((), ())), preferred_element_type=F32)


def _split_bf16(x):
    hi = x.astype(BF16)
    lo = (x - hi.astype(F32)).astype(BF16)
    return hi, lo


def _params(sem, limit=VMEM_LIMIT):
    return pltpu.CompilerParams(dimension_semantics=sem, vmem_limit_bytes=limit)


def _ada_kernel(c_ref, w_ref, b_ref, o_ref):
    a = _silu(c_ref[...])
    a_hi, a_lo = _split_bf16(a)
    w_hi, w_lo = _split_bf16(w_ref[...])
    o_ref[...] = _dot(a_hi, w_hi) + _dot(a_lo, w_hi) + _dot(a_hi, w_lo) + b_ref[...]


def _ada(cc, w, b):
    rows, d = cc.shape
    cols = w.shape[1]
    tn = 1024
    return pl.pallas_call(
        _ada_kernel,
        grid=(cols // tn,),
        in_specs=[pl.BlockSpec((rows, d), lambda j: (0, 0)),
                  pl.BlockSpec((d, tn), lambda j: (0, j)),
                  pl.BlockSpec((1, tn), lambda j: (0, j))],
        out_specs=pl.BlockSpec((rows, tn), lambda j: (0, j)),
        out_shape=jax.ShapeDtypeStruct((rows, cols), F32),
        compiler_params=_params(("parallel",)),
        name="ada",
    )(cc, w, b)


def _modulated_norm(x, g, shift, scale):
    y = x * lax.rsqrt(jnp.mean(x * x, axis=-1, keepdims=True) + EPS)
    return y * g * (1.0 + scale) + shift


def _normmm_kernel(h_ref, sh_ref, sc_ref, g_ref, w_ref, o_ref, *, tn):
    u = _modulated_norm(h_ref[...], g_ref[...], sh_ref[...], sc_ref[...]).astype(BF16)
    for j in range(o_ref.shape[-1] // tn):
        o_ref[:, j * tn:(j + 1) * tn] = _dot(u, w_ref[:, j * tn:(j + 1) * tn])


def _normmm(h, shift, scale, g, w, tm):
    bsz, l, d = h.shape
    c = w.shape[1]
    return pl.pallas_call(
        functools.partial(_normmm_kernel, tn=512),
        grid=(bsz, l // tm),
        in_specs=[pl.BlockSpec((None, tm, d), lambda b, i: (b, i, 0)),
                  pl.BlockSpec((None, 1, d), lambda b, i: (b, 0, 0)),
                  pl.BlockSpec((None, 1, d), lambda b, i: (b, 0, 0)),
                  pl.BlockSpec((1, d), lambda b, i: (0, 0)),
                  pl.BlockSpec((d, c), lambda b, i: (0, 0))],
        out_specs=pl.BlockSpec((None, tm, c), lambda b, i: (b, i, 0)),
        out_shape=jax.ShapeDtypeStruct((bsz, l, c), F32),
        compiler_params=_params(("parallel", "parallel")),
        name="normmm",
    )(h, shift, scale, g, w)


def _scan_kernel(qf_ref, ff_ref, vf_ref, qb_ref, fb_ref, vb_ref, cff_ref, cfb_ref, cv_ref,
                 lbf_ref, lbb_ref, of_ref, ob_ref, sf_ref, sb_ref, *, tl, ctx_len):
    j = pl.program_id(2)
    c_ = CHUNK
    row = lax.broadcasted_iota(I32, (c_, c_), 0)
    col = lax.broadcasted_iota(I32, (c_, c_), 1)
    tril = col <= row
    triu = col >= row
    tril_b = tril.astype(BF16)
    triu_b = triu.astype(BF16)
    lbf = lbf_ref[...]
    lbb = lbb_ref[...]
    qscale = HG_DK ** -0.5

    def decays(fz, lb, fwd):
        f = lb + (1.0 - lb) * _sigmoid(fz)
        hi, lo = _split_bf16(jnp.log(f))
        tri = tril_b if fwd else triu_b
        b = _dot(tri, hi) + _dot(tri, lo)
        if fwd:
            return f, b, b[c_ // 2 - 1:c_ // 2], b[c_ - 1:c_]
        return f, b, b[c_ // 2:c_ // 2 + 1], b[0:1]

    def next_state(s_t, f, b, b_last, v):
        kd = (1.0 - f) * jnp.exp(b_last - b)
        return jnp.exp(b_last) * s_t + _dot(v.T.astype(BF16), kd.astype(BF16))

    def chunk(q, fz, v, lb, s_t, fwd):
        f, b, b_mid, b_last = decays(fz, lb, fwd)
        qs = q * qscale
        qt = qs * jnp.exp(b - b_mid)
        kt = (1.0 - f) * jnp.exp(b_mid - b)
        a = _dot_nt(qt.astype(BF16), kt.astype(BF16))
        a = jnp.where(tril if fwd else triu, a, 0.0)
        o = _dot(a.astype(BF16), v.astype(BF16))
        o = o + _dot_nt((qs * jnp.exp(b)).astype(BF16), s_t.astype(BF16))
        return o, next_state(s_t, f, b, b_last, v)

    @pl.when(j == 0)
    def _():
        s_f = jnp.zeros((HG_DK, HG_DK), F32)
        s_b = jnp.zeros((HG_DK, HG_DK), F32)
        n = ctx_len // c_
        for c in range(n):
            f, b, _, b_last = decays(cff_ref[c * c_:(c + 1) * c_, :], lbf, True)
            s_f = next_state(s_f, f, b, b_last, cv_ref[c * c_:(c + 1) * c_, :])
            cb = n - 1 - c
            f, b, _, b_last = decays(cfb_ref[cb * c_:(cb + 1) * c_, :], lbb, False)
            s_b = next_state(s_b, f, b, b_last, cv_ref[cb * c_:(cb + 1) * c_, :])
        sf_ref[...] = s_f
        sb_ref[...] = s_b

    n_chunks = tl // c_

    def step(c, carry):
        r = pl.multiple_of(c * c_, c_)
        o, s = chunk(qf_ref[pl.ds(r, c_), :], ff_ref[pl.ds(r, c_), :], vf_ref[pl.ds(r, c_), :],
                     lbf, sf_ref[...], True)
        of_ref[pl.ds(r, c_), :] = o
        sf_ref[...] = s
        rb = pl.multiple_of((n_chunks - 1 - c) * c_, c_)
        o, s = chunk(qb_ref[pl.ds(rb, c_), :], fb_ref[pl.ds(rb, c_), :], vb_ref[pl.ds(rb, c_), :],
                     lbb, sb_ref[...], False)
        ob_ref[pl.ds(rb, c_), :] = o
        sb_ref[...] = s
        return carry

    lax.fori_loop(0, n_chunks, step, 0)


def _scan(z, zc, lb_f, lb_b, tl):
    bsz, l, _ = z.shape
    ctx_len = zc.shape[1]
    nj = l // tl
    hb = HG_HEADS
    blk = lambda colgrp, rev: pl.BlockSpec(
        (None, tl, HG_DK),
        (lambda b, h, j: (b, nj - 1 - j, colgrp * hb + h)) if rev else (lambda b, h, j: (b, j, colgrp * hb + h)))
    cblk = lambda colgrp: pl.BlockSpec((None, ctx_len, HG_DK), lambda b, h, j: (b, 0, colgrp * hb + h))
    lbspec = pl.BlockSpec((1, HG_DK), lambda b, h, j: (0, h))
    ospec = lambda rev: pl.BlockSpec(
        (None, tl, HG_DK), (lambda b, h, j: (b, nj - 1 - j, h)) if rev else (lambda b, h, j: (b, j, h)))
    return pl.pallas_call(
        functools.partial(_scan_kernel, tl=tl, ctx_len=ctx_len),
        grid=(bsz, hb, nj),
        in_specs=[blk(0, False), blk(1, False), blk(3, False),
                  blk(0, True), blk(2, True), blk(3, True),
                  cblk(0), cblk(1), cblk(2), lbspec, lbspec],
        out_specs=[ospec(False), ospec(True)],
        out_shape=[jax.ShapeDtypeStruct((bsz, l, HG_WIDTH), F32)] * 2,
        scratch_shapes=[pltpu.VMEM((HG_DK, HG_DK), F32), pltpu.VMEM((HG_DK, HG_DK), F32)],
        compiler_params=_params(("parallel", "parallel", "arbitrary")),
        name="scan",
    )(z, z, z, z, z, z, zc, zc, zc, lb_f, lb_b)


def _cdft_kernel(x_ref, w_ref, o_ref):
    o_ref[...] = _dot(x_ref[...].astype(BF16), w_ref[...]).astype(BF16)


def _cdft(z, wc, tm):
    bsz, l, _ = z.shape
    ftblk = 5 * HG_WIDTH // FT_WIDTH
    return pl.pallas_call(
        _cdft_kernel,
        grid=(bsz, l // tm),
        in_specs=[pl.BlockSpec((None, tm, FT_WIDTH), lambda b, i: (b, i, ftblk)),
                  pl.BlockSpec((FT_WIDTH, 2 * FT_WIDTH), lambda b, i: (0, 0))],
        out_specs=pl.BlockSpec((tm, 2 * FT_WIDTH), lambda b, i: (i, b)),
        out_shape=jax.ShapeDtypeStruct((l, bsz * 2 * FT_WIDTH), BF16),
        compiler_params=_params(("parallel", "parallel")),
        name="cdft",
    )(z, wc)


def _pdft_kernel(cl_ref, sl_ref, xc_ref, xs_ref, o_ref):
    o_ref[...] = (_dot(cl_ref[...], xc_ref[...]) + _dot(sl_ref[...], xs_ref[...])).astype(BF16)


def _pdft(cl, snl, r, tm):
    l = cl.shape[0]
    nb = r.shape[1] // (2 * FT_WIDTH)
    return pl.pallas_call(
        _pdft_kernel,
        grid=(l // tm, nb),
        in_specs=[pl.BlockSpec((tm, l), lambda i, n: (i, 0)),
                  pl.BlockSpec((tm, l), lambda i, n: (i, 0)),
                  pl.BlockSpec((l, FT_WIDTH), lambda i, n: (0, 2 * n)),
                  pl.BlockSpec((l, FT_WIDTH), lambda i, n: (0, 2 * n + 1))],
        out_specs=pl.BlockSpec((tm, FT_WIDTH), lambda i, n: (i, n)),
        out_shape=jax.ShapeDtypeStruct((l, nb * FT_WIDTH), BF16),
        compiler_params=_params(("parallel", "arbitrary")),
        name="pdft",
    )(cl, snl, r, r)


def _dft_tables(l):
    w = FT_GROUP_W
    jk = (jnp.arange(w, dtype=I32)[:, None] * jnp.arange(w, dtype=I32)[None, :]) % w
    ang = jk.astype(F32) * (2.0 * math.pi / w)
    norm = 1.0 / math.sqrt(l * w)
    eye = jnp.eye(FT_GROUPS, dtype=F32)
    wc = jnp.concatenate([jnp.kron(eye, jnp.cos(ang) * norm), jnp.kron(eye, jnp.sin(ang) * norm)], axis=1)
    jl = (jnp.arange(l, dtype=I32)[:, None] * jnp.arange(l, dtype=I32)[None, :]) % l
    angl = jl.astype(F32) * (2.0 * math.pi / l)
    return wc.astype(BF16), jnp.cos(angl).astype(BF16), (-jnp.sin(angl)).astype(BF16)


def _merge_kernel(of_ref, ob_ref, zg_ref, zgh_ref, zgf_ref, yft_ref, x_ref, g1_ref, hgn_ref,
                  whg_ref, wft_ref, wout_ref, o_ref):
    o = of_ref[...] + ob_ref[...]
    hgn = hgn_ref[...]
    parts = []
    for h in range(HG_HEADS):
        oh = o[:, h * HG_DK:(h + 1) * HG_DK]
        ms = jnp.mean(oh * oh, axis=-1, keepdims=True)
        parts.append(oh * lax.rsqrt(ms + EPS) * hgn[:, h * HG_DK:(h + 1) * HG_DK])
    on = jnp.concatenate(parts, axis=1) * _silu(zg_ref[...])
    y_hg = _dot(on.astype(BF16), whg_ref[...])
    y_ft = _dot(yft_ref[...], wft_ref[...])
    y = _sigmoid(zgh_ref[...]) * y_hg + _sigmoid(zgf_ref[...]) * y_ft
    o_ref[...] = x_ref[...] + g1_ref[...] * _dot(y.astype(BF16), wout_ref[...])


def _merge(o_f, o_b, z, yft, x, g1, hgn, whg, wft, wout, tm):
    bsz, l, d = x.shape
    full = lambda shape: pl.BlockSpec(shape, lambda b, i: (0,) * len(shape))
    return pl.pallas_call(
        _merge_kernel,
        grid=(bsz, l // tm),
        in_specs=[pl.BlockSpec((None, tm, HG_WIDTH), lambda b, i: (b, i, 0)),
                  pl.BlockSpec((None, tm, HG_WIDTH), lambda b, i: (b, i, 0)),
                  pl.BlockSpec((None, tm, HG_WIDTH), lambda b, i: (b, i, 4)),
                  pl.BlockSpec((None, tm, d), lambda b, i: (b, i, 3)),
                  pl.BlockSpec((None, tm, d), lambda b, i: (b, i, 4)),
                  pl.BlockSpec((tm, FT_WIDTH), lambda b, i: (i, b)),
                  pl.BlockSpec((None, tm, d), lambda b, i: (b, i, 0)),
                  pl.BlockSpec((None, 1, d), lambda b, i: (b, 0, 0)),
                  full((1, HG_WIDTH)), full((HG_WIDTH, d)), full((FT_WIDTH, d)), full((d, d))],
        out_specs=pl.BlockSpec((None, tm, d), lambda b, i: (b, i, 0)),
        out_shape=jax.ShapeDtypeStruct((bsz, l, d), F32),
        compiler_params=_params(("parallel", "parallel")),
        name="merge",
    )(o_f, o_b, z, z, z, yft, x, g1, hgn, whg, wft, wout)


def _topk_rows(s, k, payload=None):
    n, t = s.shape
    iota = lax.broadcasted_iota(I32, (n, t), 0)
    sub = lax.broadcasted_iota(I32, (SUBLANES, t), 0)
    vals, sel = [], []
    for _ in range(k):
        v = [s[SUBLANES * r:SUBLANES * (r + 1)] for r in range(n // SUBLANES)]
        g = list(range(n // SUBLANES))
        while len(v) > 1:
            nv, ng = [], []
            for a in range(0, len(v) - 1, 2):
                nv.append(jnp.maximum(v[a], v[a + 1]))
                ng.append(jnp.where(v[a] >= v[a + 1], g[a], g[a + 1]))
            if len(v) % 2:
                nv.append(v[-1])
                ng.append(g[-1])
            v, g = nv, ng
        m = jnp.max(v[0], axis=0, keepdims=True)
        am = jnp.min(jnp.where(v[0] == m, g[0] * SUBLANES + sub, n), axis=0, keepdims=True)
        hit = iota == am
        vals.append(m)
        sel.append(am if payload is None else jnp.max(jnp.where(hit, payload, -1), axis=0, keepdims=True))
        s = jnp.where(hit, -jnp.inf, s)
    return jnp.concatenate(vals, axis=0), jnp.concatenate(sel, axis=0)


def _staircase():
    return [(a, PEER_TOPK // (a + 1)) for a in range(PEER_TOPK)]


def _route_kernel(h_ref, sh_ref, sc_ref, g_ref, wq_ref, keys_ref, u_ref, e_ref, gate_ref):
    u = _modulated_norm(h_ref[...], g_ref[...], sh_ref[...], sc_ref[...])
    u_ref[...] = u
    q = _dot(u.astype(BF16), wq_ref[...]).astype(BF16)
    k_ = PEER_TOPK
    e_rows, g_rows = [], []
    for h in range(PEER_HEADS):
        sv, si = [], []
        for p in range(2):
            hp = 2 * h + p
            s_t = _dot_nt(keys_ref[hp], q[:, hp * LANES:(hp + 1) * LANES])
            v, i = _topk_rows(s_t, k_)
            sv.append(v)
            si.append(i)
        stairs = _staircase()
        pad = -sum(nb for _, nb in stairs) % SUBLANES
        tm = q.shape[0]
        cand = jnp.concatenate([sv[0][a:a + 1] + sv[1][:nb] for a, nb in stairs]
                               + [jnp.full((pad, tm), -jnp.inf, F32)], axis=0)
        cidx = jnp.concatenate([si[0][a:a + 1] * PEER_NKEYS + si[1][:nb] for a, nb in stairs]
                               + [jnp.zeros((pad, tm), I32)], axis=0)
        cv, ce = _topk_rows(cand, k_, payload=cidx)
        ex = jnp.exp(cv - cv[0:1])
        g_rows.append(ex / jnp.sum(ex, axis=0, keepdims=True))
        e_rows.append(ce)
    e_ref[...] = (jnp.concatenate(e_rows, axis=0) * ROW_WORDS).T
    gate_ref[...] = jnp.concatenate(g_rows, axis=0).T


def _route(h1, sh, sc, g, wq, keys, tm):
    bsz, l, d = h1.shape
    full = lambda shape: pl.BlockSpec(shape, lambda b, i: (0,) * len(shape))
    return pl.pallas_call(
        _route_kernel,
        grid=(bsz, l // tm),
        in_specs=[pl.BlockSpec((None, tm, d), lambda b, i: (b, i, 0)),
                  pl.BlockSpec((None, 1, d), lambda b, i: (b, 0, 0)),
                  pl.BlockSpec((None, 1, d), lambda b, i: (b, 0, 0)),
                  full((1, d)), full(wq.shape), full(keys.shape)],
        out_specs=[pl.BlockSpec((None, tm, d), lambda b, i: (b, i, 0)),
                   pl.BlockSpec((None, tm, PEER_SLOTS), lambda b, i: (b, i, 0)),
                   pl.BlockSpec((None, tm, PEER_SLOTS), lambda b, i: (b, i, 0))],
        out_shape=[jax.ShapeDtypeStruct((bsz, l, d), F32),
                   jax.ShapeDtypeStruct((bsz, l, PEER_SLOTS), I32),
                   jax.ShapeDtypeStruct((bsz, l, PEER_SLOTS), F32)],
        compiler_params=_params(("parallel", "parallel")),
        name="route",
    )(h1, sh, sc, g, wq, keys)


def _pack_table(tab):
    e, d = tab.shape
    half = d // 2
    bits = lax.bitcast_convert_type(tab.astype(BF16), jnp.uint16).astype(jnp.uint32)
    word = bits[:, :half] | (bits[:, half:] << 16)
    return lax.bitcast_convert_type(word, I32).reshape(e * ROW_WORDS, LANES)


def _gathered_tile(e_ref, t, tab_ref, kt):
    rows = []
    for j in range(kt * SLOTS_PER_TILE, (kt + 1) * SLOTS_PER_TILE):
        first = pl.multiple_of(e_ref[t, j], ROW_WORDS)
        rows.append(tab_ref[pl.ds(first, ROW_WORDS), :])
    return pltpu.bitcast(jnp.concatenate(rows, axis=0), BF16)


def _element_row_code(r):
    return 2 * (r & (ROW_WORDS - 1)) + ((r >> 2) & 1)


def _index_scratch(tb):
    return [pltpu.SMEM((tb, PEER_SLOTS), I32)] * 2 + [pltpu.SemaphoreType.DMA((2,))]


def _for_each_token(e_hbm, e_bufs, sems, tb, body):
    i = pl.program_id(0)
    n = pl.num_programs(0)

    def copy(step, slot):
        return pltpu.make_async_copy(e_hbm.at[pl.ds(step * tb, tb)], e_bufs[slot], sems.at[slot])

    @pl.when(i == 0)
    def _():
        copy(0, 0).start()

    for slot in range(2):
        @pl.when((i & 1) == slot)
        def _(slot=slot):
            copy(i, slot).wait()

            @pl.when(i + 1 < n)
            def _():
                copy(i + 1, 1 - slot).start()

            for t in range(tb):
                body(e_bufs[slot], t)


def _act_kernel(e_hbm, gate_ref, x_ref, tab_ref, w_ref, p_ref, e_a, e_b, sems, *, tb):
    tile_cols = SUBLANES * SLOTS_PER_TILE
    row = lax.broadcasted_iota(I32, (2 * SUBLANES, tile_cols), 0)
    col = lax.broadcasted_iota(I32, (2 * SUBLANES, tile_cols), 1)
    wanted = _element_row_code(row) == (col & (SUBLANES - 1))

    def token(e_ref, t):
        xt = x_ref[t]
        x_hi = xt.astype(BF16).astype(F32)
        lhs = jnp.concatenate([x_hi, xt - x_hi], axis=0).astype(BF16)
        for kt in range(PEER_SLOTS // SLOTS_PER_TILE):
            prod = _dot_nt(lhs, _gathered_tile(e_ref, t, tab_ref, kt))
            p_ref[t:t + 1, kt * tile_cols:(kt + 1) * tile_cols] = jnp.sum(
                jnp.where(wanted, prod, 0.0), axis=0, keepdims=True)

    _for_each_token(e_hbm, (e_a, e_b), sems, tb, token)
    pr = lax.broadcasted_iota(I32, (SUBLANES * PEER_SLOTS, PEER_SLOTS), 0)
    pc = lax.broadcasted_iota(I32, (SUBLANES * PEER_SLOTS, PEER_SLOTS), 1)
    fold = ((pr >> 3) == pc).astype(BF16)
    p_hi, p_lo = _split_bf16(p_ref[...])
    act = _dot(p_hi, fold) + _dot(p_lo, fold)
    w_ref[...] = gate_ref[...] * (0.5 * act * (1.0 + lax.erf(act * (2.0 ** -0.5))))


def _act(eidx, gates, x8, tab, tb):
    n = eidx.shape[0]
    return pl.pallas_call(
        functools.partial(_act_kernel, tb=tb),
        grid=(n // tb,),
        in_specs=[pl.BlockSpec(memory_space=pl.ANY),
                  pl.BlockSpec((tb, PEER_SLOTS), lambda i: (i, 0)),
                  pl.BlockSpec((tb, SUBLANES, LANES), lambda i: (i, 0, 0)),
                  pl.BlockSpec(tab.shape, lambda i: (0, 0), pipeline_mode=pl.Buffered(1))],
        out_specs=pl.BlockSpec((tb, PEER_SLOTS), lambda i: (i, 0)),
        out_shape=jax.ShapeDtypeStruct((n, PEER_SLOTS), F32),
        scratch_shapes=[pltpu.VMEM((tb, SUBLANES * PEER_SLOTS), F32)] + _index_scratch(tb),
        compiler_params=_params(("arbitrary",)),
        name="peer_act",
    )(eidx, gates, x8, tab)


def _out_kernel(e_hbm, w_ref, h_ref, g2_ref, fg_ref, tab_ref, o_ref, whi_ref, wlo_ref, e_a, e_b, sems, *, tb):
    er = lax.broadcasted_iota(I32, (PEER_SLOTS, SUBLANES * PEER_SLOTS), 0)
    ec = lax.broadcasted_iota(I32, (PEER_SLOTS, SUBLANES * PEER_SLOTS), 1)
    spread = ((ec >> 3) == er).astype(BF16)
    w_hi, w_lo = _split_bf16(w_ref[...])
    whi_ref[...] = _dot(w_hi, spread)
    wlo_ref[...] = _dot(w_lo, spread)
    tile_cols = SUBLANES * SLOTS_PER_TILE
    row = lax.broadcasted_iota(I32, (2 * SUBLANES, tile_cols), 0)
    col = lax.broadcasted_iota(I32, (2 * SUBLANES, tile_cols), 1)
    wanted = _element_row_code(row) == (col & (SUBLANES - 1))

    def token(e_ref, t):
        acc = jnp.zeros((2 * SUBLANES, LANES), F32)
        for kt in range(PEER_SLOTS // SLOTS_PER_TILE):
            a = whi_ref[t:t + 1, kt * tile_cols:(kt + 1) * tile_cols]
            b = wlo_ref[t:t + 1, kt * tile_cols:(kt + 1) * tile_cols]
            lhs = jnp.where(wanted, jnp.where(row < SUBLANES, a, b), 0.0).astype(BF16)
            acc = acc + _dot(lhs, _gathered_tile(e_ref, t, tab_ref, kt))
        o_ref[t] = acc[:SUBLANES] + acc[SUBLANES:]

    _for_each_token(e_hbm, (e_a, e_b), sems, tb, token)
    h2 = h_ref[...] + g2_ref[...] * o_ref[...]
    ms = jnp.sum(jnp.sum(h2 * h2, axis=2, keepdims=True), axis=1, keepdims=True) * (1.0 / (SUBLANES * LANES))
    o_ref[...] = h2 * lax.rsqrt(ms + EPS) * fg_ref[...]


def _peer_out(eidx, w, h8, g2_8, fg8, tab, tb, tokens_per_batch):
    n = eidx.shape[0]
    return pl.pallas_call(
        functools.partial(_out_kernel, tb=tb),
        grid=(n // tb,),
        in_specs=[pl.BlockSpec(memory_space=pl.ANY),
                  pl.BlockSpec((tb, PEER_SLOTS), lambda i: (i, 0)),
                  pl.BlockSpec((tb, SUBLANES, LANES), lambda i: (i, 0, 0)),
                  pl.BlockSpec((None, SUBLANES, LANES), lambda i: ((i * tb) // tokens_per_batch, 0, 0)),
                  pl.BlockSpec((SUBLANES, LANES), lambda i: (0, 0)),
                  pl.BlockSpec(tab.shape, lambda i: (0, 0), pipeline_mode=pl.Buffered(1))],
        out_specs=pl.BlockSpec((tb, SUBLANES, LANES), lambda i: (i, 0, 0)),
        out_shape=jax.ShapeDtypeStruct((n, SUBLANES, LANES), F32),
        scratch_shapes=[pltpu.VMEM((tb, SUBLANES * PEER_SLOTS), F32)] * 2 + _index_scratch(tb),
        compiler_params=_params(("arbitrary",)),
        name="peer_out",
    )(eidx, w, h8, g2_8, fg8, tab)


def kernel(x, c, ctx, c_ctx, w_ada, b_ada, norm_mix_g, norm_ffn_g, w_in, hg_lb_f, hg_lb_b, hg_norm_g,
           w_hg_out, w_ft_out, w_out, peer_w_q, peer_sub_keys, peer_u, peer_v, final_norm_g):
    bsz, l, d = x.shape
    n = bsz * l
    layer = 0

    lb_f = jnp.cumsum(jax.nn.softmax(hg_lb_f.astype(F32), axis=0), axis=0)[layer][None, :]
    lb_b = jnp.cumsum(jax.nn.softmax(hg_lb_b.astype(F32), axis=0), axis=0)[layer][None, :]

    rows = -(-(bsz + 1) // SUBLANES) * SUBLANES
    cc = jnp.zeros((rows, d), F32).at[:bsz].set(c).at[bsz].set(c_ctx)
    mod = _ada(cc, w_ada[layer], b_ada[layer][None, :])
    sh1, sc1, g1, sh2, sc2, g2 = [mod[:bsz, k * d:(k + 1) * d][:, None, :] for k in range(6)]
    sh1c, sc1c = [jnp.broadcast_to(mod[bsz, k * d:(k + 1) * d][None, None, :], (bsz, 1, d)) for k in range(2)]

    g_mix = norm_mix_g[layer][None, :]
    w_in_b = w_in[layer].astype(BF16)
    z = _normmm(x, sh1, sc1, g_mix, w_in_b, tm=256)
    zc = _normmm(ctx, sh1c, sc1c, g_mix, w_in_b[:, HG_WIDTH:4 * HG_WIDTH], tm=ctx.shape[1])

    o_f, o_b = _scan(z, zc, lb_f, lb_b, tl=512)

    wc, cl, snl = _dft_tables(l)
    yft = _pdft(cl, snl, _cdft(z, wc, tm=512), tm=512)

    h1 = _merge(o_f, o_b, z, yft, x, g1, hg_norm_g[layer].reshape(1, HG_WIDTH),
                w_hg_out[layer].astype(BF16), w_ft_out[layer].astype(BF16), w_out[layer].astype(BF16), tm=512)

    keys = peer_sub_keys[layer].reshape(PEER_HEADS * 2, PEER_NKEYS, -1).astype(BF16)
    u2, eidx, gates = _route(h1, sh2, sc2, norm_ffn_g[layer][None, :], peer_w_q[layer].astype(BF16), keys, tm=256)

    eidx = eidx.reshape(n, PEER_SLOTS)
    w = _act(eidx, gates.reshape(n, PEER_SLOTS), u2.reshape(n, SUBLANES, LANES), _pack_table(peer_u[layer]),
             tb=PEER_TB)
    out = _peer_out(eidx, w, h1.reshape(n, SUBLANES, LANES), g2.reshape(bsz, SUBLANES, LANES),
                    final_norm_g.reshape(SUBLANES, LANES), _pack_table(peer_v[layer]), tb=PEER_TB, tokens_per_batch=l)
    return out.reshape(bsz, l, d)
```

```python
import functools
import math

import jax
import jax.numpy as jnp
from jax import lax
from jax.experimental import pallas as pl
from jax.experimental.pallas import tpu as pltpu

F32 = jnp.float32
BF16 = jnp.bfloat16
I32 = jnp.int32

EPS = 1e-6
HG_HEADS = 4
HG_DK = 128
HG_WIDTH = HG_HEADS * HG_DK
CHUNK = 64
FT_GROUPS = 4
FT_GROUP_W = 128
FT_WIDTH = FT_GROUPS * FT_GROUP_W
PEER_HEADS = 8
PEER_NKEYS = 128
PEER_TOPK = 16
PEER_SLOTS = PEER_HEADS * PEER_TOPK
LANES = 128
SUBLANES = 8
ROW_WORDS = 4
SLOTS_PER_TILE = 32
PEER_TB = 128
VMEM_LIMIT = 56 * 1024 * 1024


def _sigmoid(x):
    return 1.0 / (1.0 + jnp.exp(-x))


def _silu(x):
    return x * _sigmoid(x)


def _dot(a, b):
    return jnp.dot(a, b, preferred_element_type=F32)


def _dot_nt(a, b):
    return lax.dot_general(a, b, (((1,), (1,)), ((), ())), preferred_element_type=F32)


def _split_bf16(x):
    hi = x.astype(BF16)
    lo = (x - hi.astype(F32)).astype(BF16)
    return hi, lo


def _params(sem, limit=VMEM_LIMIT):
    return pltpu.CompilerParams(dimension_semantics=sem, vmem_limit_bytes=limit)


def _ada_kernel(c_ref, w_ref, b_ref, o_ref):
    a = _silu(c_ref[...])
    a_hi, a_lo = _split_bf16(a)
    w_hi, w_lo = _split_bf16(w_ref[...])
    o_ref[...] = _dot(a_hi, w_hi) + _dot(a_lo, w_hi) + _dot(a_hi, w_lo) + b_ref[...]


def _ada(cc, w, b):
    rows, d = cc.shape
    cols = w.shape[1]
    tn = 1024
    return pl.pallas_call(
        _ada_kernel,
        grid=(cols // tn,),
        in_specs=[pl.BlockSpec((rows, d), lambda j: (0, 0)),
                  pl.BlockSpec((d, tn), lambda j: (0, j)),
                  pl.BlockSpec((1, tn), lambda j: (0, j))],
        out_specs=pl.BlockSpec((rows, tn), lambda j: (0, j)),
        out_shape=jax.ShapeDtypeStruct((rows, cols), F32),
        compiler_params=_params(("parallel",)),
        name="ada",
    )(cc, w, b)


def _modulated_norm(x, g, shift, scale):
    y = x * lax.rsqrt(jnp.mean(x * x, axis=-1, keepdims=True) + EPS)
    return y * g * (1.0 + scale) + shift


def _normmm_kernel(h_ref, sh_ref, sc_ref, g_ref, w_ref, o_ref, *, tn):
    u = _modulated_norm(h_ref[...], g_ref[...], sh_ref[...], sc_ref[...]).astype(BF16)
    for j in range(o_ref.shape[-1] // tn):
        o_ref[:, j * tn:(j + 1) * tn] = _dot(u, w_ref[:, j * tn:(j + 1) * tn])


def _normmm(h, shift, scale, g, w, tm):
    bsz, l, d = h.shape
    c = w.shape[1]
    return pl.pallas_call(
        functools.partial(_normmm_kernel, tn=512),
        grid=(bsz, l // tm),
        in_specs=[pl.BlockSpec((None, tm, d), lambda b, i: (b, i, 0)),
                  pl.BlockSpec((None, 1, d), lambda b, i: (b, 0, 0)),
                  pl.BlockSpec((None, 1, d), lambda b, i: (b, 0, 0)),
                  pl.BlockSpec((1, d), lambda b, i: (0, 0)),
                  pl.BlockSpec((d, c), lambda b, i: (0, 0))],
        out_specs=pl.BlockSpec((None, tm, c), lambda b, i: (b, i, 0)),
        out_shape=jax.ShapeDtypeStruct((bsz, l, c), F32),
        compiler_params=_params(("parallel", "parallel")),
        name="normmm",
    )(h, shift, scale, g, w)


def _scan_kernel(qf_ref, ff_ref, vf_ref, qb_ref, fb_ref, vb_ref, cff_ref, cfb_ref, cv_ref,
                 lbf_ref, lbb_ref, of_ref, ob_ref, sf_ref, sb_ref, *, tl, ctx_len):
    j = pl.program_id(2)
    c_ = CHUNK
    row = lax.broadcasted_iota(I32, (c_, c_), 0)
    col = lax.broadcasted_iota(I32, (c_, c_), 1)
    tril = col <= row
    triu = col >= row
    tril_b = tril.astype(BF16)
    triu_b = triu.astype(BF16)
    lbf = lbf_ref[...]
    lbb = lbb_ref[...]
    qscale = HG_DK ** -0.5

    def decays(fz, lb, fwd):
        f = lb + (1.0 - lb) * _sigmoid(fz)
        hi, lo = _split_bf16(jnp.log(f))
        tri = tril_b if fwd else triu_b
        b = _dot(tri, hi) + _dot(tri, lo)
        if fwd:
            return f, b, b[c_ // 2 - 1:c_ // 2], b[c_ - 1:c_]
        return f, b, b[c_ // 2:c_ // 2 + 1], b[0:1]

    def next_state(s_t, f, b, b_last, v):
        kd = (1.0 - f) * jnp.exp(b_last - b)
        return jnp.exp(b_last) * s_t + _dot(v.T.astype(BF16), kd.astype(BF16))

    def chunk(q, fz, v, lb, s_t, fwd):
        f, b, b_mid, b_last = decays(fz, lb, fwd)
        qs = q * qscale
        qt = qs * jnp.exp(b - b_mid)
        kt = (1.0 - f) * jnp.exp(b_mid - b)
        a = _dot_nt(qt.astype(BF16), kt.astype(BF16))
        a = jnp.where(tril if fwd else triu, a, 0.0)
        o = _dot(a.astype(BF16), v.astype(BF16))
        o = o + _dot_nt((qs * jnp.exp(b)).astype(BF16), s_t.astype(BF16))
        return o, next_state(s_t, f, b, b_last, v)

    @pl.when(j == 0)
    def _():
        s_f = jnp.zeros((HG_DK, HG_DK), F32)
        s_b = jnp.zeros((HG_DK, HG_DK), F32)
        n = ctx_len // c_
        for c in range(n):
            f, b, _, b_last = decays(cff_ref[c * c_:(c + 1) * c_, :], lbf, True)
            s_f = next_state(s_f, f, b, b_last, cv_ref[c * c_:(c + 1) * c_, :])
            cb = n - 1 - c
            f, b, _, b_last = decays(cfb_ref[cb * c_:(cb + 1) * c_, :], lbb, False)
            s_b = next_state(s_b, f, b, b_last, cv_ref[cb * c_:(cb + 1) * c_, :])
        sf_ref[...] = s_f
        sb_ref[...] = s_b

    n_chunks = tl // c_

    def step(c, carry):
        r = pl.multiple_of(c * c_, c_)
        o, s = chunk(qf_ref[pl.ds(r, c_), :], ff_ref[pl.ds(r, c_), :], vf_ref[pl.ds(r, c_), :],
                     lbf, sf_ref[...], True)
        of_ref[pl.ds(r, c_), :] = o
        sf_ref[...] = s
        rb = pl.multiple_of((n_chunks - 1 - c) * c_, c_)
        o, s = chunk(qb_ref[pl.ds(rb, c_), :], fb_ref[pl.ds(rb, c_), :], vb_ref[pl.ds(rb, c_), :],
                     lbb, sb_ref[...], False)
        ob_ref[pl.ds(rb, c_), :] = o
        sb_ref[...] = s
        return carry

    lax.fori_loop(0, n_chunks, step, 0)


def _scan(z, zc, lb_f, lb_b, tl):
    bsz, l, _ = z.shape
    ctx_len = zc.shape[1]
    nj = l // tl
    hb = HG_HEADS
    blk = lambda colgrp, rev: pl.BlockSpec(
        (None, tl, HG_DK),
        (lambda b, h, j: (b, nj - 1 - j, colgrp * hb + h)) if rev else (lambda b, h, j: (b, j, colgrp * hb + h)))
    cblk = lambda colgrp: pl.BlockSpec((None, ctx_len, HG_DK), lambda b, h, j: (b, 0, colgrp * hb + h))
    lbspec = pl.BlockSpec((1, HG_DK), lambda b, h, j: (0, h))
    ospec = lambda rev: pl.BlockSpec(
        (None, tl, HG_DK), (lambda b, h, j: (b, nj - 1 - j, h)) if rev else (lambda b, h, j: (b, j, h)))
    return pl.pallas_call(
        functools.partial(_scan_kernel, tl=tl, ctx_len=ctx_len),
        grid=(bsz, hb, nj),
        in_specs=[blk(0, False), blk(1, False), blk(3, False),
                  blk(0, True), blk(2, True), blk(3, True),
                  cblk(0), cblk(1), cblk(2), lbspec, lbspec],
        out_specs=[ospec(False), ospec(True)],
        out_shape=[jax.ShapeDtypeStruct((bsz, l, HG_WIDTH), F32)] * 2,
        scratch_shapes=[pltpu.VMEM((HG_DK, HG_DK), F32), pltpu.VMEM((HG_DK, HG_DK), F32)],
        compiler_params=_params(("parallel", "parallel", "arbitrary")),
        name="scan",
    )(z, z, z, z, z, z, zc, zc, zc, lb_f, lb_b)


def _cdft_kernel(x_ref, w_ref, o_ref):
    o_ref[...] = _dot(x_ref[...].astype(BF16), w_ref[...]).astype(BF16)


def _cdft(z, wc, tm):
    bsz, l, _ = z.shape
    ftblk = 5 * HG_WIDTH // FT_WIDTH
    return pl.pallas_call(
        _cdft_kernel,
        grid=(bsz, l // tm),
        in_specs=[pl.BlockSpec((None, tm, FT_WIDTH), lambda b, i: (b, i, ftblk)),
                  pl.BlockSpec((FT_WIDTH, 2 * FT_WIDTH), lambda b, i: (0, 0))],
        out_specs=pl.BlockSpec((tm, 2 * FT_WIDTH), lambda b, i: (i, b)),
        out_shape=jax.ShapeDtypeStruct((l, bsz * 2 * FT_WIDTH), BF16),
        compiler_params=_params(("parallel", "parallel")),
        name="cdft",
    )(z, wc)


def _pdft_kernel(cl_ref, sl_ref, xc_ref, xs_ref, o_ref):
    o_ref[...] = (_dot(cl_ref[...], xc_ref[...]) + _dot(sl_ref[...], xs_ref[...])).astype(BF16)


def _pdft(cl, snl, r, tm):
    l = cl.shape[0]
    nb = r.shape[1] // (2 * FT_WIDTH)
    return pl.pallas_call(
        _pdft_kernel,
        grid=(l // tm, nb),
        in_specs=[pl.BlockSpec((tm, l), lambda i, n: (i, 0)),
                  pl.BlockSpec((tm, l), lambda i, n: (i, 0)),
                  pl.BlockSpec((l, FT_WIDTH), lambda i, n: (0, 2 * n)),
                  pl.BlockSpec((l, FT_WIDTH), lambda i, n: (0, 2 * n + 1))],
        out_specs=pl.BlockSpec((tm, FT_WIDTH), lambda i, n: (i, n)),
        out_shape=jax.ShapeDtypeStruct((l, nb * FT_WIDTH), BF16),
        compiler_params=_params(("parallel", "arbitrary")),
        name="pdft",
    )(cl, snl, r, r)


def _dft_tables(l):
    w = FT_GROUP_W
    jk = (jnp.arange(w, dtype=I32)[:, None] * jnp.arange(w, dtype=I32)[None, :]) % w
    ang = jk.astype(F32) * (2.0 * math.pi / w)
    norm = 1.0 / math.sqrt(l * w)
    eye = jnp.eye(FT_GROUPS, dtype=F32)
    wc = jnp.concatenate([jnp.kron(eye, jnp.cos(ang) * norm), jnp.kron(eye, jnp.sin(ang) * norm)], axis=1)
    jl = (jnp.arange(l, dtype=I32)[:, None] * jnp.arange(l, dtype=I32)[None, :]) % l
    angl = jl.astype(F32) * (2.0 * math.pi / l)
    return wc.astype(BF16), jnp.cos(angl).astype(BF16), (-jnp.sin(angl)).astype(BF16)


def _merge_kernel(of_ref, ob_ref, zg_ref, zgh_ref, zgf_ref, yft_ref, x_ref, g1_ref, hgn_ref,
                  whg_ref, wft_ref, wout_ref, o_ref):
    o = of_ref[...] + ob_ref[...]
    hgn = hgn_ref[...]
    parts = []
    for h in range(HG_HEADS):
        oh = o[:, h * HG_DK:(h + 1) * HG_DK]
        ms = jnp.mean(oh * oh, axis=-1, keepdims=True)
        parts.append(oh * lax.rsqrt(ms + EPS) * hgn[:, h * HG_DK:(h + 1) * HG_DK])
    on = jnp.concatenate(parts, axis=1) * _silu(zg_ref[...])
    y_hg = _dot(on.astype(BF16), whg_ref[...])
    y_ft = _dot(yft_ref[...], wft_ref[...])
    y = _sigmoid(zgh_ref[...]) * y_hg + _sigmoid(zgf_ref[...]) * y_ft
    o_ref[...] = x_ref[...] + g1_ref[...] * _dot(y.astype(BF16), wout_ref[...])


def _merge(o_f, o_b, z, yft, x, g1, hgn, whg, wft, wout, tm):
    bsz, l, d = x.shape
    full = lambda shape: pl.BlockSpec(shape, lambda b, i: (0,) * len(shape))
    return pl.pallas_call(
        _merge_kernel,
        grid=(bsz, l // tm),
        in_specs=[pl.BlockSpec((None, tm, HG_WIDTH), lambda b, i: (b, i, 0)),
                  pl.BlockSpec((None, tm, HG_WIDTH), lambda b, i: (b, i, 0)),
                  pl.BlockSpec((None, tm, HG_WIDTH), lambda b, i: (b, i, 4)),
                  pl.BlockSpec((None, tm, d), lambda b, i: (b, i, 3)),
                  pl.BlockSpec((None, tm, d), lambda b, i: (b, i, 4)),
                  pl.BlockSpec((tm, FT_WIDTH), lambda b, i: (i, b)),
                  pl.BlockSpec((None, tm, d), lambda b, i: (b, i, 0)),
                  pl.BlockSpec((None, 1, d), lambda b, i: (b, 0, 0)),
                  full((1, HG_WIDTH)), full((HG_WIDTH, d)), full((FT_WIDTH, d)), full((d, d))],
        out_specs=pl.BlockSpec((None, tm, d), lambda b, i: (b, i, 0)),
        out_shape=jax.ShapeDtypeStruct((bsz, l, d), F32),
        compiler_params=_params(("parallel", "parallel")),
        name="merge",
    )(o_f, o_b, z, z, z, yft, x, g1, hgn, whg, wft, wout)


def _topk_rows(s, k, payload=None):
    n, t = s.shape
    iota = lax.broadcasted_iota(I32, (n, t), 0)
    sub = lax.broadcasted_iota(I32, (SUBLANES, t), 0)
    vals, sel = [], []
    for _ in range(k):
        v = [s[SUBLANES * r:SUBLANES * (r + 1)] for r in range(n // SUBLANES)]
        g = list(range(n // SUBLANES))
        while len(v) > 1:
            nv, ng = [], []
            for a in range(0, len(v) - 1, 2):
                nv.append(jnp.maximum(v[a], v[a + 1]))
                ng.append(jnp.where(v[a] >= v[a + 1], g[a], g[a + 1]))
            if len(v) % 2:
                nv.append(v[-1])
                ng.append(g[-1])
            v, g = nv, ng
        m = jnp.max(v[0], axis=0, keepdims=True)
        am = jnp.min(jnp.where(v[0] == m, g[0] * SUBLANES + sub, n), axis=0, keepdims=True)
        hit = iota == am
        vals.append(m)
        sel.append(am if payload is None else jnp.max(jnp.where(hit, payload, -1), axis=0, keepdims=True))
        s = jnp.where(hit, -jnp.inf, s)
    return jnp.concatenate(vals, axis=0), jnp.concatenate(sel, axis=0)


def _staircase():
    return [(a, PEER_TOPK // (a + 1)) for a in range(PEER_TOPK)]


def _route_kernel(h_ref, sh_ref, sc_ref, g_ref, wq_ref, keys_ref, u_ref, e_ref, gate_ref):
    u = _modulated_norm(h_ref[...], g_ref[...], sh_ref[...], sc_ref[...])
    u_ref[...] = u
    q = _dot(u.astype(BF16), wq_ref[...]).astype(BF16)
    k_ = PEER_TOPK
    e_rows, g_rows = [], []
    for h in range(PEER_HEADS):
        sv, si = [], []
        for p in range(2):
            hp = 2 * h + p
            s_t = _dot_nt(keys_ref[hp], q[:, hp * LANES:(hp + 1) * LANES])
            v, i = _topk_rows(s_t, k_)
            sv.append(v)
            si.append(i)
        stairs = _staircase()
        pad = -sum(nb for _, nb in stairs) % SUBLANES
        tm = q.shape[0]
        cand = jnp.concatenate([sv[0][a:a + 1] + sv[1][:nb] for a, nb in stairs]
                               + [jnp.full((pad, tm), -jnp.inf, F32)], axis=0)
        cidx = jnp.concatenate([si[0][a:a + 1] * PEER_NKEYS + si[1][:nb] for a, nb in stairs]
                               + [jnp.zeros((pad, tm), I32)], axis=0)
        cv, ce = _topk_rows(cand, k_, payload=cidx)
        ex = jnp.exp(cv - cv[0:1])
        g_rows.append(ex / jnp.sum(ex, axis=0, keepdims=True))
        e_rows.append(ce)
    e_ref[...] = (jnp.concatenate(e_rows, axis=0) * ROW_WORDS).T
    gate_ref[...] = jnp.concatenate(g_rows, axis=0).T


def _route(h1, sh, sc, g, wq, keys, tm):
    bsz, l, d = h1.shape
    full = lambda shape: pl.BlockSpec(shape, lambda b, i: (0,) * len(shape))
    return pl.pallas_call(
        _route_kernel,
        grid=(bsz, l // tm),
        in_specs=[pl.BlockSpec((None, tm, d), lambda b, i: (b, i, 0)),
                  pl.BlockSpec((None, 1, d), lambda b, i: (b, 0, 0)),
                  pl.BlockSpec((None, 1, d), lambda b, i: (b, 0, 0)),
                  full((1, d)), full(wq.shape), full(keys.shape)],
        out_specs=[pl.BlockSpec((None, tm, d), lambda b, i: (b, i, 0)),
                   pl.BlockSpec((None, tm, PEER_SLOTS), lambda b, i: (b, i, 0)),
                   pl.BlockSpec((None, tm, PEER_SLOTS), lambda b, i: (b, i, 0))],
        out_shape=[jax.ShapeDtypeStruct((bsz, l, d), F32),
                   jax.ShapeDtypeStruct((bsz, l, PEER_SLOTS), I32),
                   jax.ShapeDtypeStruct((bsz, l, PEER_SLOTS), F32)],
        compiler_params=_params(("parallel", "parallel")),
        name="route",
    )(h1, sh, sc, g, wq, keys)


def _pack_table(tab):
    e, d = tab.shape
    half = d // 2
    bits = lax.bitcast_convert_type(tab.astype(BF16), jnp.uint16).astype(jnp.uint32)
    word = bits[:, :half] | (bits[:, half:] << 16)
    return lax.bitcast_convert_type(word, I32).reshape(e * ROW_WORDS, LANES)


def _gathered_tile(e_ref, t, tab_ref, kt):
    rows = []
    for j in range(kt * SLOTS_PER_TILE, (kt + 1) * SLOTS_PER_TILE):
        first = pl.multiple_of(e_ref[t, j], ROW_WORDS)
        rows.append(tab_ref[pl.ds(first, ROW_WORDS), :])
    return pltpu.bitcast(jnp.concatenate(rows, axis=0), BF16)


def _element_row_code(r):
    return 2 * (r & (ROW_WORDS - 1)) + ((r >> 2) & 1)


def _index_scratch(tb):
    return [pltpu.SMEM((tb, PEER_SLOTS), I32)] * 2 + [pltpu.SemaphoreType.DMA((2,))]


def _for_each_token(e_hbm, e_bufs, sems, tb, body):
    i = pl.program_id(0)
    n = pl.num_programs(0)

    def copy(step, slot):
        return pltpu.make_async_copy(e_hbm.at[pl.ds(step * tb, tb)], e_bufs[slot], sems.at[slot])

    @pl.when(i == 0)
    def _():
        copy(0, 0).start()

    for slot in range(2):
        @pl.when((i & 1) == slot)
        def _(slot=slot):
            copy(i, slot).wait()

            @pl.when(i + 1 < n)
            def _():
                copy(i + 1, 1 - slot).start()

            for t in range(tb):
                body(e_bufs[slot], t)


def _act_kernel(e_hbm, gate_ref, x_ref, tab_ref, w_ref, p_ref, e_a, e_b, sems, *, tb):
    tile_cols = SUBLANES * SLOTS_PER_TILE
    row = lax.broadcasted_iota(I32, (2 * SUBLANES, tile_cols), 0)
    col = lax.broadcasted_iota(I32, (2 * SUBLANES, tile_cols), 1)
    wanted = _element_row_code(row) == (col & (SUBLANES - 1))

    def token(e_ref, t):
        xt = x_ref[t]
        x_hi = xt.astype(BF16).astype(F32)
        lhs = jnp.concatenate([x_hi, xt - x_hi], axis=0).astype(BF16)
        for kt in range(PEER_SLOTS // SLOTS_PER_TILE):
            prod = _dot_nt(lhs, _gathered_tile(e_ref, t, tab_ref, kt))
            p_ref[t:t + 1, kt * tile_cols:(kt + 1) * tile_cols] = jnp.sum(
                jnp.where(wanted, prod, 0.0), axis=0, keepdims=True)

    _for_each_token(e_hbm, (e_a, e_b), sems, tb, token)
    pr = lax.broadcasted_iota(I32, (SUBLANES * PEER_SLOTS, PEER_SLOTS), 0)
    pc = lax.broadcasted_iota(I32, (SUBLANES * PEER_SLOTS, PEER_SLOTS), 1)
    fold = ((pr >> 3) == pc).astype(BF16)
    p_hi, p_lo = _split_bf16(p_ref[...])
    act = _dot(p_hi, fold) + _dot(p_lo, fold)
    w_ref[...] = gate_ref[...] * (0.5 * act * (1.0 + lax.erf(act * (2.0 ** -0.5))))


def _act(eidx, gates, x8, tab, tb):
    n = eidx.shape[0]
    return pl.pallas_call(
        functools.partial(_act_kernel, tb=tb),
        grid=(n // tb,),
        in_specs=[pl.BlockSpec(memory_space=pl.ANY),
                  pl.BlockSpec((tb, PEER_SLOTS), lambda i: (i, 0)),
                  pl.BlockSpec((tb, SUBLANES, LANES), lambda i: (i, 0, 0)),
                  pl.BlockSpec(tab.shape, lambda i: (0, 0), pipeline_mode=pl.Buffered(1))],
        out_specs=pl.BlockSpec((tb, PEER_SLOTS), lambda i: (i, 0)),
        out_shape=jax.ShapeDtypeStruct((n, PEER_SLOTS), F32),
        scratch_shapes=[pltpu.VMEM((tb, SUBLANES * PEER_SLOTS), F32)] + _index_scratch(tb),
        compiler_params=_params(("arbitrary",)),
        name="peer_act",
    )(eidx, gates, x8, tab)


def _out_kernel(e_hbm, w_ref, h_ref, g2_ref, fg_ref, tab_ref, o_ref, whi_ref, wlo_ref, e_a, e_b, sems, *, tb):
    er = lax.broadcasted_iota(I32, (PEER_SLOTS, SUBLANES * PEER_SLOTS), 0)
    ec = lax.broadcasted_iota(I32, (PEER_SLOTS, SUBLANES * PEER_SLOTS), 1)
    spread = ((ec >> 3) == er).astype(BF16)
    w_hi, w_lo = _split_bf16(w_ref[...])
    whi_ref[...] = _dot(w_hi, spread)
    wlo_ref[...] = _dot(w_lo, spread)
    tile_cols = SUBLANES * SLOTS_PER_TILE
    row = lax.broadcasted_iota(I32, (2 * SUBLANES, tile_cols), 0)
    col = lax.broadcasted_iota(I32, (2 * SUBLANES, tile_cols), 1)
    wanted = _element_row_code(row) == (col & (SUBLANES - 1))

    def token(e_ref, t):
        acc = jnp.zeros((2 * SUBLANES, LANES), F32)
        for kt in range(PEER_SLOTS // SLOTS_PER_TILE):
            a = whi_ref[t:t + 1, kt * tile_cols:(kt + 1) * tile_cols]
            b = wlo_ref[t:t + 1, kt * tile_cols:(kt + 1) * tile_cols]
            lhs = jnp.where(wanted, jnp.where(row < SUBLANES, a, b), 0.0).astype(BF16)
            acc = acc + _dot(lhs, _gathered_tile(e_ref, t, tab_ref, kt))
        o_ref[t] = acc[:SUBLANES] + acc[SUBLANES:]

    _for_each_token(e_hbm, (e_a, e_b), sems, tb, token)
    h2 = h_ref[...] + g2_ref[...] * o_ref[...]
    ms = jnp.sum(jnp.sum(h2 * h2, axis=2, keepdims=True), axis=1, keepdims=True) * (1.0 / (SUBLANES * LANES))
    o_ref[...] = h2 * lax.rsqrt(ms + EPS) * fg_ref[...]


def _peer_out(eidx, w, h8, g2_8, fg8, tab, tb, tokens_per_batch):
    n = eidx.shape[0]
    return pl.pallas_call(
        functools.partial(_out_kernel, tb=tb),
        grid=(n // tb,),
        in_specs=[pl.BlockSpec(memory_space=pl.ANY),
                  pl.BlockSpec((tb, PEER_SLOTS), lambda i: (i, 0)),
                  pl.BlockSpec((tb, SUBLANES, LANES), lambda i: (i, 0, 0)),
                  pl.BlockSpec((None, SUBLANES, LANES), lambda i: ((i * tb) // tokens_per_batch, 0, 0)),
                  pl.BlockSpec((SUBLANES, LANES), lambda i: (0, 0)),
                  pl.BlockSpec(tab.shape, lambda i: (0, 0), pipeline_mode=pl.Buffered(1))],
        out_specs=pl.BlockSpec((tb, SUBLANES, LANES), lambda i: (i, 0, 0)),
        out_shape=jax.ShapeDtypeStruct((n, SUBLANES, LANES), F32),
        scratch_shapes=[pltpu.VMEM((tb, SUBLANES * PEER_SLOTS), F32)] * 2 + _index_scratch(tb),
        compiler_params=_params(("arbitrary",)),
        name="peer_out",
    )(eidx, w, h8, g2_8, fg8, tab)


def kernel(x, c, ctx, c_ctx, w_ada, b_ada, norm_mix_g, norm_ffn_g, w_in, hg_lb_f, hg_lb_b, hg_norm_g,
           w_hg_out, w_ft_out, w_out, peer_w_q, peer_sub_keys, peer_u, peer_v, final_norm_g):
    bsz, l, d = x.shape
    n = bsz * l
    layer = 0

    lb_f = jnp.cumsum(jax.nn.softmax(hg_lb_f.astype(F32), axis=0), axis=0)[layer][None, :]
    lb_b = jnp.cumsum(jax.nn.softmax(hg_lb_b.astype(F32), axis=0), axis=0)[layer][None, :]

    rows = -(-(bsz + 1) // SUBLANES) * SUBLANES
    cc = jnp.zeros((rows, d), F32).at[:bsz].set(c).at[bsz].set(c_ctx)
    mod = _ada(cc, w_ada[layer], b_ada[layer][None, :])
    sh1, sc1, g1, sh2, sc2, g2 = [mod[:bsz, k * d:(k + 1) * d][:, None, :] for k in range(6)]
    sh1c, sc1c = [jnp.broadcast_to(mod[bsz, k * d:(k + 1) * d][None, None, :], (bsz, 1, d)) for k in range(2)]

    g_mix = norm_mix_g[layer][None, :]
    w_in_b = w_in[layer].astype(BF16)
    z = _normmm(x, sh1, sc1, g_mix, w_in_b, tm=256)
    zc = _normmm(ctx, sh1c, sc1c, g_mix, w_in_b[:, HG_WIDTH:4 * HG_WIDTH], tm=ctx.shape[1])

    o_f, o_b = _scan(z, zc, lb_f, lb_b, tl=512)

    wc, cl, snl = _dft_tables(l)
    yft = _pdft(cl, snl, _cdft(z, wc, tm=512), tm=512)

    h1 = _merge(o_f, o_b, z, yft, x, g1, hg_norm_g[layer].reshape(1, HG_WIDTH),
                w_hg_out[layer].astype(BF16), w_ft_out[layer].astype(BF16), w_out[layer].astype(BF16), tm=512)

    keys = peer_sub_keys[layer].reshape(PEER_HEADS * 2, PEER_NKEYS, -1).astype(BF16)
    u2, eidx, gates = _route(h1, sh2, sc2, norm_ffn_g[layer][None, :], peer_w_q[layer].astype(BF16), keys, tm=256)

    eidx = eidx.reshape(n, PEER_SLOTS)
    w = _act(eidx, gates.reshape(n, PEER_SLOTS), u2.reshape(n, SUBLANES, LANES), _pack_table(peer_u[layer]),
             tb=PEER_TB)
    out = _peer_out(eidx, w, h1.reshape(n, SUBLANES, LANES), g2.reshape(bsz, SUBLANES, LANES),
                    final_norm_g.reshape(SUBLANES, LANES), _pack_table(peer_v[layer]), tb=PEER_TB, tokens_per_batch=l)
    return out.reshape(bsz, l, d)
```
